```python
import math
import jax, jax.numpy as jnp
from jax import lax
import numpy as np

D_MODEL = 1024
BATCH = 8
SEQ = 2048
DEPTH = 2

N_A = DEPTH // 2
N_B = DEPTH - N_A
HG_DK = 128
HG_HEADS = D_MODEL // HG_DK
HG_DV = D_MODEL // HG_HEADS
HG_WIDTH = HG_HEADS * HG_DK
HG_VWIDTH = HG_HEADS * HG_DV
HG_CHUNK = 64
ATT_HD = 64
ATT_QH = D_MODEL // ATT_HD
ATT_KVH = 4
ATT_G = ATT_QH // ATT_KVH
WINDOW = 128
FFN_HIDDEN = -(-8 * D_MODEL // (3 * 256)) * 256
PLE_DIM = 256
DEEPNORM_ALPHA = (2.0 * DEPTH) ** 0.25
DEEPNORM_BETA = (8.0 * DEPTH) ** -0.25
LN_EPS = 1e-5
RMS_EPS = 1e-6

kernel_name = 'yoco_hgrn2_swa_sink_alibi_deepnorm'


def layer_norm(x, g, b):
    xf = x.astype(jnp.float32)
    mu = jnp.mean(xf, axis=-1, keepdims=True)
    var = jnp.mean(jnp.square(xf - mu), axis=-1, keepdims=True)
    return ((xf - mu) * lax.rsqrt(var + LN_EPS) * g.astype(jnp.float32) + b.astype(jnp.float32)).astype(x.dtype)


def rms_norm(x, g):
    xf = x.astype(jnp.float32)
    return xf * lax.rsqrt(jnp.mean(jnp.square(xf), axis=-1, keepdims=True) + RMS_EPS) * g.astype(jnp.float32)


def hgrn2_mixer(x, w_in, lb, norm_gain, w_out):
    B, S, _ = x.shape
    f32 = jnp.float32
    proj = x @ w_in
    q = proj[..., :HG_WIDTH]
    f = proj[..., HG_WIDTH:2 * HG_WIDTH].astype(f32)
    v = proj[..., 2 * HG_WIDTH:2 * HG_WIDTH + HG_VWIDTH].astype(f32)
    g = proj[..., 2 * HG_WIDTH + HG_VWIDTH:]
    lb = lb.astype(f32)
    forget = lb + (1.0 - lb) * jax.nn.sigmoid(f)
    log_f = jnp.log(forget)
    k = (1.0 - lb) * jax.nn.sigmoid(-f)
    q = jax.nn.silu(q.astype(f32)) * (HG_DK ** -0.5)
    nc = S // HG_CHUNK

    def to_chunks(t, d):
        return t.reshape(B, nc, HG_CHUNK, HG_HEADS, d).transpose(1, 0, 3, 2, 4)

    causal = jnp.tril(jnp.ones((HG_CHUNK, HG_CHUNK), dtype=bool))[:, :, None]

    def step(state, inp):
        qc, kc, gc, vc = inp
        b = jnp.cumsum(gc, axis=2)
        diff = b[:, :, :, None, :] - b[:, :, None, :, :]
        decay = jnp.exp(jnp.where(causal, diff, -jnp.inf))
        scores = jnp.einsum('bhtc,bhsc,bhtsc->bhts', qc, kc, decay)
        o = jnp.einsum('bhts,bhsv->bhtv', scores, vc) + jnp.einsum('bhtc,bhcv->bhtv', qc * jnp.exp(b), state)
        b_last = b[:, :, -1:, :]
        state = jnp.exp(b_last[:, :, 0, :, None]) * state + jnp.einsum('bhsc,bhsv->bhcv', kc * jnp.exp(b_last - b), vc)
        return state, o

    s0 = jnp.zeros((B, HG_HEADS, HG_DK, HG_DV), f32)
    _, o = lax.scan(step, s0, (to_chunks(q, HG_DK), to_chunks(k, HG_DK), to_chunks(log_f, HG_DK), to_chunks(v, HG_DV)))
    o = o.transpose(1, 0, 3, 2, 4).reshape(B, S, HG_HEADS, HG_DV)
    o = rms_norm(o, norm_gain) * jax.nn.silu(g.astype(f32)).reshape(B, S, HG_HEADS, HG_DV)
    return o.reshape(B, S, HG_VWIDTH).astype(x.dtype) @ w_out


def shared_kv(h, kv_w, kv_b):
    B, S, _ = h.shape
    kv = h @ kv_w + kv_b
    kdim = ATT_KVH * ATT_HD
    return kv[..., :kdim].reshape(B, S, ATT_KVH, ATT_HD), kv[..., kdim:].reshape(B, S, ATT_KVH, ATT_HD)


def band_blocks(t):
    B, S = t.shape[:2]
    tp = jnp.pad(t, ((0, 0), (WINDOW, 0), (0, 0), (0, 0)))
    tr = tp.reshape(B, S // WINDOW + 1, WINDOW, ATT_KVH, ATT_HD)
    return jnp.concatenate([tr[:, :-1], tr[:, 1:]], axis=2)


def alibi_slopes(n_heads):
    return jnp.exp2(-8.0 * jnp.arange(1, n_heads + 1, dtype=jnp.float32) / n_heads)


def swa_sink_mixer(x, k_blocks, v_blocks, w_q, b_q, sinks, w_out, b_out):
    B, S, _ = x.shape
    nb = S // WINDOW
    f32 = jnp.float32
    q = (x @ w_q + b_q).reshape(B, nb, WINDOW, ATT_KVH, ATT_G, ATT_HD)
    s = jnp.einsum('bnqkgd,bnskd->bnkgqs', q, k_blocks).astype(f32) * (ATT_HD ** -0.5)
    qi = jnp.arange(WINDOW)[:, None]
    si = jnp.arange(2 * WINDOW)[None, :]
    dist = qi - si + WINDOW
    blk = jnp.arange(nb)[:, None, None]
    valid = (dist >= 0) & (dist < WINDOW) & (blk * WINDOW - WINDOW + si >= 0)
    slopes = alibi_slopes(ATT_QH).reshape(ATT_KVH, ATT_G)
    s = s - slopes[:, :, None, None] * dist.astype(f32)
    s = jnp.where(valid[None, :, None, None], s, -jnp.inf)
    sink = jnp.broadcast_to(sinks.astype(f32).reshape(ATT_KVH, ATT_G)[None, None, :, :, None, None], s.shape[:-1] + (1,))
    probs = jax.nn.softmax(jnp.concatenate([s, sink], axis=-1), axis=-1)[..., :-1]
    o = jnp.einsum('bnkgqs,bnskd->bnqkgd', probs.astype(v_blocks.dtype), v_blocks).reshape(B, S, ATT_QH * ATT_HD)
    return o @ w_out + b_out


def swiglu(x, w_gate_up, w_down):
    gu = x @ w_gate_up
    return (jax.nn.silu(gu[..., :FFN_HIDDEN]) * gu[..., FFN_HIDDEN:]) @ w_down


def per_layer_embedding(x, p_i, w_up, w_gate, b_gate):
    return jax.nn.sigmoid(x @ w_gate + b_gate) * (p_i @ w_up)


def setup_inputs(seed: int = 0) -> dict:
    key = jax.random.key(seed)
    ks = jax.random.split(key, 24)
    D = D_MODEL
    nrm = lambda k, shape, scale: jax.random.normal(k, shape, jnp.float32) * scale
    kvd = ATT_KVH * ATT_HD
    qd = ATT_QH * ATT_HD
    return {
        'x': nrm(ks[0], (BATCH, SEQ, D), 1.0),
        'p': nrm(ks[1], (DEPTH, BATCH, SEQ, PLE_DIM), 1.0),
        'a_w_in': jnp.concatenate([
            nrm(ks[2], (N_A, D, 2 * HG_WIDTH), D ** -0.5),
            nrm(ks[3], (N_A, D, HG_VWIDTH), D ** -0.5 * DEEPNORM_BETA),
            nrm(ks[4], (N_A, D, HG_VWIDTH), D ** -0.5)], axis=-1),
        'a_lower_bound': nrm(ks[5], (N_A + 1, HG_WIDTH), 0.5),
        'a_norm_gain': 1.0 + nrm(ks[6], (N_A, HG_DV), 0.05),
        'a_w_out': nrm(ks[7], (N_A, HG_VWIDTH, D), HG_VWIDTH ** -0.5 * DEEPNORM_BETA),
        'kv_w': jnp.concatenate([
            nrm(ks[8], (D, kvd), D ** -0.5),
            nrm(ks[9], (D, kvd), D ** -0.5 * DEEPNORM_BETA)], axis=-1),
        'kv_b': nrm(ks[10], (2 * kvd,), 0.02),
        'b_w_q': nrm(ks[11], (N_B, D, qd), D ** -0.5),
        'b_b_q': nrm(ks[12], (N_B, qd), 0.02),
        'b_sinks': nrm(ks[13], (N_B, ATT_QH), 1.0),
        'b_w_out': nrm(ks[14], (N_B, qd, D), qd ** -0.5 * DEEPNORM_BETA),
        'b_b_out': nrm(ks[15], (N_B, D), 0.02),
        'ffn_w_gate_up': nrm(ks[16], (DEPTH, D, 2 * FFN_HIDDEN), D ** -0.5),
        'ffn_w_down': nrm(ks[17], (DEPTH, FFN_HIDDEN, D), FFN_HIDDEN ** -0.5 * DEEPNORM_BETA),
        'ple_w_up': nrm(ks[18], (DEPTH, PLE_DIM, D), PLE_DIM ** -0.5 * DEEPNORM_BETA),
        'ple_w_gate': nrm(ks[19], (DEPTH, D, D), D ** -0.5),
        'ple_b_gate': nrm(ks[20], (DEPTH, D), 0.02),
        'ln_gain': 1.0 + nrm(ks[21], (DEPTH, 3, D), 0.05),
        'ln_bias': nrm(ks[22], (DEPTH, 3, D), 0.02),
    }


def reference(x, p, a_w_in, a_lower_bound, a_norm_gain, a_w_out, kv_w, kv_b, b_w_q, b_b_q, b_sinks, b_w_out, b_b_out,
              ffn_w_gate_up, ffn_w_down, ple_w_up, ple_w_gate, ple_b_gate, ln_gain, ln_bias):
    lower_bounds = jnp.cumsum(jax.nn.softmax(a_lower_bound.astype(jnp.float32), axis=0), axis=0)
    k_blocks = None
    v_blocks = None
    for i in range(DEPTH):
        if i < N_A:
            h = hgrn2_mixer(x, a_w_in[i], lower_bounds[i], a_norm_gain[i], a_w_out[i])
        else:
            if i == N_A:
                k_sh, v_sh = shared_kv(x, kv_w, kv_b)
                k_blocks = band_blocks(k_sh)
                v_blocks = band_blocks(v_sh)
            j = i - N_A
            h = swa_sink_mixer(x, k_blocks, v_blocks, b_w_q[j], b_b_q[j], b_sinks[j], b_w_out[j], b_b_out[j])
        x = layer_norm(DEEPNORM_ALPHA * x + h, ln_gain[i, 0], ln_bias[i, 0])
        x = layer_norm(DEEPNORM_ALPHA * x + swiglu(x, ffn_w_gate_up[i], ffn_w_down[i]), ln_gain[i, 1], ln_bias[i, 1])
        x = layer_norm(DEEPNORM_ALPHA * x + per_layer_embedding(x, p[i], ple_w_up[i], ple_w_gate[i], ple_b_gate[i]),
                       ln_gain[i, 2], ln_bias[i, 2])
    return x
```

```python
import functools

import jax
import jax.numpy as jnp
from jax import lax
from jax.experimental import pallas as pl
from jax.experimental.pallas import tpu as pltpu

F32 = jnp.float32
BF16 = jnp.bfloat16

D_MODEL = 1024
DEPTH = 2
HG_DK = 128
HG_HEADS = 8
HG_WIDTH = HG_HEADS * HG_DK
ATT_HD = 64
ATT_QH = 16
ATT_KVH = 4
ATT_G = ATT_QH // ATT_KVH
WINDOW = 128
FFN_HIDDEN = 2816
PLE_DIM = 256
DEEPNORM_ALPHA = (2.0 * DEPTH) ** 0.25
LN_EPS = 1e-5
RMS_EPS = 1e-6

LANES = 128
VMEM_LIMIT_BYTES = 56 * 1024 * 1024

HG_CHUNK = 128
HG_LEVELS = (64, 32, 16, 8, 4, 2, 1)
FFN_ROWS = 256
SWA_ROWS = 256


def _dot(a, b):
    return jnp.dot(a, b, preferred_element_type=F32)


def _dot_nt(a, b):
    return lax.dot_general(a, b, (((1,), (1,)), ((), ())), preferred_element_type=F32)


def _dot_tn(a, b):
    return lax.dot_general(a, b, (((0,), (0,)), ((), ())), preferred_element_type=F32)


def _sigmoid(x):
    return 1.0 / (1.0 + jnp.exp(-x))


def _layer_norm(x, g, b):
    mu = jnp.mean(x, axis=-1, keepdims=True)
    xc = x - mu
    var = jnp.mean(xc * xc, axis=-1, keepdims=True)
    return xc * lax.rsqrt(var + LN_EPS) * g + b


def _hgrn_level_exponent(n, b_h, lf_h, row, sign):
    c = b_h.shape[0]
    if n >= 4:
        b3 = b_h.reshape(c // (2 * n), 2 * n, HG_DK)
        d = (b3 - b3[:, n - 1:n, :]).reshape(c, HG_DK)
        return d * sign
    if n == 2:
        p = row & 3
        nxt = pltpu.roll(lf_h, c - 1, axis=0)
        prv = pltpu.roll(lf_h, 1, axis=0)
        zero = jnp.zeros_like(lf_h)
        return (jnp.where(p == 0, nxt, zero) + jnp.where(p >= 2, lf_h, zero)
                + jnp.where(p == 3, prv, zero))
    return jnp.where((row & 1) == 1, lf_h, jnp.zeros_like(lf_h))


def _hgrn_layer_kernel(x_ref, w_in_ref, lb_ref, gain_ref, w_out_ref, lng_ref, lnb_ref,
                       o_ref, st_ref, y_ref):
    c = HG_CHUNK

    @pl.when(pl.program_id(1) == 0)
    def _():
        st_ref[...] = jnp.zeros_like(st_ref)

    x = x_ref[0]
    proj = _dot(x.astype(BF16), w_in_ref[...])
    lb = lb_ref[...]
    one_m_lb = 1.0 - lb
    f_raw = proj[:, HG_WIDTH:2 * HG_WIDTH]
    log_f = jnp.log(lb + one_m_lb * _sigmoid(f_raw))
    k_all = one_m_lb * _sigmoid(-f_raw)
    q_raw = proj[:, :HG_WIDTH]
    q_all = q_raw * _sigmoid(q_raw) * (HG_DK ** -0.5)
    v_all = proj[:, 2 * HG_WIDTH:3 * HG_WIDTH]
    g_raw = proj[:, 3 * HG_WIDTH:]
    g_all = g_raw * _sigmoid(g_raw)

    rr = lax.broadcasted_iota(jnp.int32, (c, c), 0)
    cc = lax.broadcasted_iota(jnp.int32, (c, c), 1)
    tri = (rr >= cc).astype(BF16)
    hi = log_f.astype(BF16)
    r1 = log_f - hi.astype(F32)
    mid = r1.astype(BF16)
    lo = (r1 - mid.astype(F32)).astype(BF16)
    b_all = _dot(tri, hi) + _dot(tri, mid) + _dot(tri, lo)

    xr = rr ^ cc
    lvl = jnp.full((c, c), -1, jnp.int32)
    for j in range(len(HG_LEVELS)):
        lvl = lvl + (xr >= (1 << j)).astype(jnp.int32)
    lvl = jnp.where(rr > cc, lvl, -1)

    row = lax.broadcasted_iota(jnp.int32, (c, HG_DK), 0)
    signs = {n: jnp.where(((row // n) & 1) == 1, 1.0, -1.0).astype(F32) for n in HG_LEVELS if n >= 4}
    gain = gain_ref[...]

    for h in range(HG_HEADS):
        sl = slice(h * HG_DK, (h + 1) * HG_DK)
        q_h, k_h, lf_h, b_h, v_h = q_all[:, sl], k_all[:, sl], log_f[:, sl], b_all[:, sl], v_all[:, sl]
        v_bf = v_h.astype(BF16)

        scores = jnp.zeros((c, c), F32)
        for n in HG_LEVELS:
            e = jnp.exp(_hgrn_level_exponent(n, b_h, lf_h, row, signs.get(n)))
            s_n = _dot_nt((q_h * e).astype(BF16), (k_h * e).astype(BF16))
            scores = jnp.where(lvl == (n.bit_length() - 1), s_n, scores)

        st = st_ref[h]
        o_h = _dot(scores.astype(BF16), v_bf)
        o_h = o_h + _dot_nt((q_h * jnp.exp(b_h)).astype(BF16), st.astype(BF16))
        o_h = o_h + jnp.sum(q_h * k_h, axis=-1, keepdims=True) * v_h

        b_last = b_h[c - 1:c, :]
        k_dec = (k_h * jnp.exp(b_last - b_h)).astype(BF16)
        st_ref[h] = st * jnp.exp(b_last) + _dot_tn(v_bf, k_dec)

        ms = jnp.mean(o_h * o_h, axis=-1, keepdims=True)
        y_h = o_h * lax.rsqrt(ms + RMS_EPS) * gain * g_all[:, sl]
        y_ref[:, sl] = y_h.astype(BF16)

    mix = _dot(y_ref[...], w_out_ref[...])
    o_ref[0] = _layer_norm(DEEPNORM_ALPHA * x + mix, lng_ref[...], lnb_ref[...])


def _const_spec(shape):
    zeros = (0,) * len(shape)
    return pl.BlockSpec(shape, lambda *_: zeros, pipeline_mode=pl.Buffered(1))


def _hgrn_layer(x, w_in, lb, gain, w_out, lng, lnb):
    bsz, seq, d = x.shape
    c = HG_CHUNK
    return pl.pallas_call(
        _hgrn_layer_kernel,
        grid=(bsz, seq // c),
        in_specs=[
            pl.BlockSpec((1, c, d), lambda b, i: (b, i, 0)),
            _const_spec(w_in.shape), _const_spec(lb.shape), _const_spec(gain.shape),
            _const_spec(w_out.shape), _const_spec(lng.shape), _const_spec(lnb.shape),
        ],
        out_specs=pl.BlockSpec((1, c, d), lambda b, i: (b, i, 0)),
        out_shape=jax.ShapeDtypeStruct(x.shape, F32),
        scratch_shapes=[pltpu.VMEM((HG_HEADS, HG_DK, HG_DK), F32), pltpu.VMEM((c, HG_WIDTH), BF16)],
        compiler_params=pltpu.CompilerParams(dimension_semantics=("arbitrary", "arbitrary"),
                                             vmem_limit_bytes=VMEM_LIMIT_BYTES),
        name="hgrn_layer",
    )(x, w_in, lb, gain, w_out, lng, lnb)


def _ffn_ple_kernel(x_ref, p_ref, wgu_ref, wd_ref, wpu_ref, wpg_ref, bpg_ref, lng_ref, lnb_ref, o_ref):
    x = x_ref[...]
    gu = _dot(x.astype(BF16), wgu_ref[...])
    gate, up = gu[:, :FFN_HIDDEN], gu[:, FFN_HIDDEN:]
    hid = (gate * _sigmoid(gate) * up).astype(BF16)
    x = _layer_norm(DEEPNORM_ALPHA * x + _dot(hid, wd_ref[...]), lng_ref[0:1, :], lnb_ref[0:1, :])
    ple_gate = _sigmoid(_dot(x.astype(BF16), wpg_ref[...]) + bpg_ref[...])
    ple_up = _dot(p_ref[...].astype(BF16), wpu_ref[...])
    o_ref[...] = _layer_norm(DEEPNORM_ALPHA * x + ple_gate * ple_up, lng_ref[1:2, :], lnb_ref[1:2, :])


def _ffn_ple(x2d, p2d, wgu, wd, wpu, wpg, bpg, lng, lnb):
    t, d = x2d.shape
    rows = FFN_ROWS
    return pl.pallas_call(
        _ffn_ple_kernel,
        grid=(t // rows,),
        in_specs=[
            pl.BlockSpec((rows, d), lambda i: (i, 0)),
            pl.BlockSpec((rows, PLE_DIM), lambda i: (i, 0)),
            _const_spec(wgu.shape), _const_spec(wd.shape), _const_spec(wpu.shape), _const_spec(wpg.shape),
            _const_spec(bpg.shape), _const_spec(lng.shape), _const_spec(lnb.shape),
        ],
        out_specs=pl.BlockSpec((rows, d), lambda i: (i, 0)),
        out_shape=jax.ShapeDtypeStruct(x2d.shape, F32),
        compiler_params=pltpu.CompilerParams(dimension_semantics=("arbitrary",),
                                             vmem_limit_bytes=VMEM_LIMIT_BYTES),
        name="ffn_ple",
    )(x2d, p2d, wgu, wd, wpu, wpg, bpg, lng, lnb)


def _swa_layer_kernel(slopes_ref, sinks_ref, x_ref, wkv_ref, bkv_ref, wq_ref, bq_ref, wo_ref, bo_ref,
                      lng_ref, lnb_ref, o_ref, kprev_ref, vprev_ref, att_ref):
    w = WINDOW
    rows = x_ref.shape[1]
    first = pl.program_id(1) == 0

    @pl.when(first)
    def _():
        kprev_ref[...] = jnp.zeros_like(kprev_ref)
        vprev_ref[...] = jnp.zeros_like(vprev_ref)

    x = x_ref[0]
    xb = x.astype(BF16)
    kv = _dot(xb, wkv_ref[...]) + bkv_ref[...]
    kd = kv[:, :ATT_KVH * LANES].astype(BF16)
    vd = kv[:, ATT_KVH * LANES:].astype(BF16)
    q = ((_dot(xb, wq_ref[...]) + bq_ref[...]) * (ATT_HD ** -0.5)).astype(BF16)

    lane = lax.broadcasted_iota(jnp.int32, (1, LANES), 1)
    lo_half = lane < ATT_HD
    qi = lax.broadcasted_iota(jnp.int32, (w, 2 * w), 0)
    si = lax.broadcasted_iota(jnp.int32, (w, 2 * w), 1)
    dist = qi - si + w
    band = (dist >= 0) & (dist < w)
    dist_f = dist.astype(F32)
    zero_bf = jnp.zeros((), BF16)

    for j in range(rows // w):
        blk = slice(j * w, (j + 1) * w)
        if j == 0:
            k_prev, v_prev = kprev_ref[...], vprev_ref[...]
            valid = band & (si >= jnp.where(first, w, 0))
        else:
            prev = slice((j - 1) * w, j * w)
            k_prev, v_prev = kd[prev], vd[prev]
            valid = band
        keys = jnp.concatenate([k_prev, kd[blk]], axis=0)
        vals = jnp.concatenate([v_prev, vd[blk]], axis=0)
        for kh in range(ATT_KVH):
            grp = slice(kh * LANES, (kh + 1) * LANES)
            k_dup = keys[:, grp]
            v_dup = vals[:, grp]
            v_lo = jnp.where(lo_half, v_dup, zero_bf)
            v_hi = jnp.where(lo_half, zero_bf, v_dup)
            for pair in range(ATT_G // 2):
                col = slice((kh * ATT_G + 2 * pair) * ATT_HD, (kh * ATT_G + 2 * pair + 2) * ATT_HD)
                q_pair = q[blk, col]
                out_pair = None
                inv = []
                for half in range(2):
                    head = kh * ATT_G + 2 * pair + half
                    q_half = jnp.where(lo_half, q_pair, zero_bf) if half == 0 else jnp.where(lo_half, zero_bf, q_pair)
                    s = _dot_nt(q_half, k_dup)
                    s = jnp.where(valid, s - slopes_ref[head] * dist_f, -jnp.inf)
                    sink = sinks_ref[head]
                    m = jnp.maximum(jnp.max(s, axis=-1, keepdims=True), sink)
                    pexp = jnp.exp(s - m)
                    den = jnp.sum(pexp, axis=-1, keepdims=True) + jnp.exp(sink - m)
                    inv.append(1.0 / den)
                    pv = _dot(pexp.astype(BF16), v_lo if half == 0 else v_hi)
                    out_pair = pv if out_pair is None else out_pair + pv
                att_ref[blk, col] = (out_pair * jnp.where(lo_half, inv[0], inv[1])).astype(BF16)

    kprev_ref[...] = kd[rows - w:]
    vprev_ref[...] = vd[rows - w:]
    mix = _dot(att_ref[...], wo_ref[...]) + bo_ref[...]
    o_ref[0] = _layer_norm(DEEPNORM_ALPHA * x + mix, lng_ref[...], lnb_ref[...])


def _swa_layer(x, slopes, sinks, wkv, bkv, wq, bq, wo, bo, lng, lnb):
    bsz, seq, d = x.shape
    rows = SWA_ROWS
    smem = pl.BlockSpec(memory_space=pltpu.SMEM)
    return pl.pallas_call(
        _swa_layer_kernel,
        grid=(bsz, seq // rows),
        in_specs=[
            smem, smem,
            pl.BlockSpec((1, rows, d), lambda b, i: (b, i, 0)),
            _const_spec(wkv.shape), _const_spec(bkv.shape), _const_spec(wq.shape), _const_spec(bq.shape),
            _const_spec(wo.shape), _const_spec(bo.shape), _const_spec(lng.shape), _const_spec(lnb.shape),
        ],
        out_specs=pl.BlockSpec((1, rows, d), lambda b, i: (b, i, 0)),
        out_shape=jax.ShapeDtypeStruct(x.shape, F32),
        scratch_shapes=[pltpu.VMEM((WINDOW, ATT_KVH * LANES), BF16), pltpu.VMEM((WINDOW, ATT_KVH * LANES), BF16),
                        pltpu.VMEM((rows, ATT_QH * ATT_HD), BF16)],
        compiler_params=pltpu.CompilerParams(dimension_semantics=("arbitrary", "arbitrary"),
                                             vmem_limit_bytes=VMEM_LIMIT_BYTES),
        name="swa_layer",
    )(slopes, sinks, x, wkv, bkv, wq, bq, wo, bo, lng, lnb)


def _dup_heads(w):
    lead = w.shape[:-1]
    w4 = w.reshape(lead + (ATT_KVH, 1, ATT_HD))
    return jnp.broadcast_to(w4, lead + (ATT_KVH, 2, ATT_HD)).reshape(lead + (ATT_KVH * 2 * ATT_HD,))


def kernel(x, p, a_w_in, a_lower_bound, a_norm_gain, a_w_out, kv_w, kv_b, b_w_q, b_b_q, b_sinks, b_w_out, b_b_out,
           ffn_w_gate_up, ffn_w_down, ple_w_up, ple_w_gate, ple_b_gate, ln_gain, ln_bias):
    bsz, seq, d = x.shape
    row = lambda v: v.reshape(1, -1).astype(F32)
    lower_bounds = jnp.cumsum(jax.nn.softmax(a_lower_bound.astype(F32), axis=0), axis=0)

    def ffn_ple(i, h):
        out = _ffn_ple(h.reshape(bsz * seq, d), p[i].reshape(bsz * seq, PLE_DIM),
                       ffn_w_gate_up[i].astype(BF16), ffn_w_down[i].astype(BF16), ple_w_up[i].astype(BF16),
                       ple_w_gate[i].astype(BF16), row(ple_b_gate[i]), ln_gain[i, 1:3], ln_bias[i, 1:3])
        return out.reshape(bsz, seq, d)

    h = _hgrn_layer(x, a_w_in[0].astype(BF16), row(lower_bounds[0]), row(a_norm_gain[0]), a_w_out[0].astype(BF16),
                    row(ln_gain[0, 0]), row(ln_bias[0, 0]))
    h = ffn_ple(0, h)

    kdim = ATT_KVH * ATT_HD
    wkv = jnp.concatenate([_dup_heads(kv_w[:, :kdim]), _dup_heads(kv_w[:, kdim:])], axis=-1).astype(BF16)
    bkv = row(jnp.concatenate([_dup_heads(kv_b[:kdim]), _dup_heads(kv_b[kdim:])], axis=-1))
    slopes = jnp.exp2(-8.0 * jnp.arange(1, ATT_QH + 1, dtype=F32) / ATT_QH)
    h = _swa_layer(h, slopes, b_sinks[0].astype(F32), wkv, bkv, b_w_q[0].astype(BF16), row(b_b_q[0]),
                   b_w_out[0].astype(BF16), row(b_b_out[0]), row(ln_gain[1, 0]), row(ln_bias[1, 0]))
    h = ffn_ple(1, h)
    return h
```

```python
import functools

import jax
import jax.numpy as jnp
from jax import lax
from jax.experimental import pallas as pl
from jax.experimental.pallas import tpu as pltpu

F32 = jnp.float32
BF16 = jnp.bfloat16

D_MODEL = 1024
DEPTH = 2
HG_DK = 128
HG_HEADS = 8
HG_WIDTH = HG_HEADS * HG_DK
ATT_HD = 64
ATT_QH = 16
ATT_KVH = 4
ATT_G = ATT_QH // ATT_KVH
WINDOW = 128
FFN_HIDDEN = 2816
PLE_DIM = 256
DEEPNORM_ALPHA = (2.0 * DEPTH) ** 0.25
LN_EPS = 1e-5
RMS_EPS = 1e-6

LANES = 128
BF16_ROWS = 16
LOG2E = 1.4426950408889634
VMEM_LIMIT_BYTES = 56 * 1024 * 1024

HG_CHUNK = 128
HG_LEVELS = (64, 32, 16, 8, 4, 2, 1)
FFN_ROWS = 256
SWA_ROWS = 256


def _dot(a, b):
    return jnp.dot(a, b, preferred_element_type=F32)


def _dot_nt(a, b):
    return lax.dot_general(a, b, (((1,), (1,)), ((), ())), preferred_element_type=F32)


def _dot_tn(a, b):
    return lax.dot_general(a, b, (((0,), (0,)), ((), ())), preferred_element_type=F32)


def _sigmoid(x):
    return 1.0 / (1.0 + jnp.exp(-x))


def _layer_norm(x, g, b):
    mu = jnp.mean(x, axis=-1, keepdims=True)
    xc = x - mu
    var = jnp.mean(xc * xc, axis=-1, keepdims=True)
    return xc * lax.rsqrt(var + LN_EPS) * g + b


def _hgrn_level_exponent(n, b_h, lf_h, row, sign):
    c = b_h.shape[0]
    if n >= 4:
        b3 = b_h.reshape(c // (2 * n), 2 * n, HG_DK)
        d = (b3 - b3[:, n - 1:n, :]).reshape(c, HG_DK)
        return d * sign
    if n == 2:
        p = row & 3
        nxt = pltpu.roll(lf_h, c - 1, axis=0)
        prv = pltpu.roll(lf_h, 1, axis=0)
        zero = jnp.zeros_like(lf_h)
        return (jnp.where(p == 0, nxt, zero) + jnp.where(p >= 2, lf_h, zero)
                + jnp.where(p == 3, prv, zero))
    return jnp.where((row & 1) == 1, lf_h, jnp.zeros_like(lf_h))


def _hgrn_layer_kernel(x_ref, w_in_ref, lb_ref, gain_ref, w_out_ref, lng_ref, lnb_ref,
                       o_ref, st_ref, y_ref):
    c = HG_CHUNK

    @pl.when(pl.program_id(1) == 0)
    def _():
        st_ref[...] = jnp.zeros_like(st_ref)

    x = x_ref[0]
    proj = _dot(x.astype(BF16), w_in_ref[...].astype(BF16))
    lb = lb_ref[...]
    one_m_lb = 1.0 - lb
    f_raw = proj[:, HG_WIDTH:2 * HG_WIDTH]
    log_f = jnp.log(lb + one_m_lb * _sigmoid(f_raw))
    k_all = one_m_lb * _sigmoid(-f_raw)
    q_raw = proj[:, :HG_WIDTH]
    q_all = q_raw * _sigmoid(q_raw) * (HG_DK ** -0.5)
    v_all = proj[:, 2 * HG_WIDTH:3 * HG_WIDTH]
    g_raw = proj[:, 3 * HG_WIDTH:]
    g_all = g_raw * _sigmoid(g_raw)

    rr = lax.broadcasted_iota(jnp.int32, (c, c), 0)
    cc = lax.broadcasted_iota(jnp.int32, (c, c), 1)
    tri = (rr >= cc).astype(BF16)
    hi = log_f.astype(BF16)
    r1 = log_f - hi.astype(F32)
    mid = r1.astype(BF16)
    lo = (r1 - mid.astype(F32)).astype(BF16)
    b_all = _dot(tri, hi) + _dot(tri, mid) + _dot(tri, lo)

    xr = rr ^ cc
    lvl = jnp.full((c, c), -1, jnp.int32)
    for j in range(len(HG_LEVELS)):
        lvl = lvl + (xr >= (1 << j)).astype(jnp.int32)
    lvl = jnp.where(rr > cc, lvl, -1)

    row = lax.broadcasted_iota(jnp.int32, (c, HG_DK), 0)
    signs = {n: jnp.where(((row // n) & 1) == 1, 1.0, -1.0).astype(F32) for n in HG_LEVELS if n >= 4}
    gain = gain_ref[...]

    for h in range(HG_HEADS):
        sl = slice(h * HG_DK, (h + 1) * HG_DK)
        q_h, k_h, lf_h, b_h, v_h = q_all[:, sl], k_all[:, sl], log_f[:, sl], b_all[:, sl], v_all[:, sl]
        v_bf = v_h.astype(BF16)

        scores = jnp.zeros((c, c), F32)
        for n in HG_LEVELS:
            e = jnp.exp(_hgrn_level_exponent(n, b_h, lf_h, row, signs.get(n)))
            s_n = _dot_nt((q_h * e).astype(BF16), (k_h * e).astype(BF16))
            scores = jnp.where(lvl == (n.bit_length() - 1), s_n, scores)

        st = st_ref[h]
        o_h = _dot(scores.astype(BF16), v_bf)
        o_h = o_h + _dot_nt((q_h * jnp.exp(b_h)).astype(BF16), st.astype(BF16))
        o_h = o_h + jnp.sum(q_h * k_h, axis=-1, keepdims=True) * v_h

        b_last = b_h[c - 1:c, :]
        k_dec = (k_h * jnp.exp(b_last - b_h)).astype(BF16)
        st_ref[h] = st * jnp.exp(b_last) + _dot_tn(v_bf, k_dec)

        ms = jnp.mean(o_h * o_h, axis=-1, keepdims=True)
        y_h = o_h * lax.rsqrt(ms + RMS_EPS) * gain * g_all[:, sl]
        y_ref[:, sl] = y_h.astype(BF16)

    mix = _dot(y_ref[...], w_out_ref[...].astype(BF16))
    o_ref[0] = _layer_norm(DEEPNORM_ALPHA * x + mix, lng_ref[...], lnb_ref[...])


def _const_spec(shape):
    zeros = (0,) * len(shape)
    return pl.BlockSpec(shape, lambda *_: zeros, pipeline_mode=pl.Buffered(1))


def _hgrn_layer(x, w_in, lb, gain, w_out, lng, lnb):
    bsz, seq, d = x.shape
    c = HG_CHUNK
    return pl.pallas_call(
        _hgrn_layer_kernel,
        grid=(bsz, seq // c),
        in_specs=[
            pl.BlockSpec((1, c, d), lambda b, i: (b, i, 0)),
            _const_spec(w_in.shape), _const_spec(lb.shape), _const_spec(gain.shape),
            _const_spec(w_out.shape), _const_spec(lng.shape), _const_spec(lnb.shape),
        ],
        out_specs=pl.BlockSpec((1, c, d), lambda b, i: (b, i, 0)),
        out_shape=jax.ShapeDtypeStruct(x.shape, F32),
        scratch_shapes=[pltpu.VMEM((HG_HEADS, HG_DK, HG_DK), F32), pltpu.VMEM((c, HG_WIDTH), BF16)],
        compiler_params=pltpu.CompilerParams(dimension_semantics=("arbitrary", "arbitrary"),
                                             vmem_limit_bytes=VMEM_LIMIT_BYTES),
        name="hgrn_layer",
    )(x, w_in, lb, gain, w_out, lng, lnb)


def _ffn_ple_kernel(x_ref, p_ref, wgu_ref, wd_ref, wpu_ref, wpg_ref, bpg_ref, lng_ref, lnb_ref, o_ref):
    x = x_ref[...]
    gu = _dot(x.astype(BF16), wgu_ref[...].astype(BF16))
    gate, up = gu[:, :FFN_HIDDEN], gu[:, FFN_HIDDEN:]
    hid = (gate * _sigmoid(gate) * up).astype(BF16)
    x = _layer_norm(DEEPNORM_ALPHA * x + _dot(hid, wd_ref[...].astype(BF16)), lng_ref[0:1, :], lnb_ref[0:1, :])
    ple_gate = _sigmoid(_dot(x.astype(BF16), wpg_ref[...].astype(BF16)) + bpg_ref[...])
    ple_up = _dot(p_ref[...].astype(BF16), wpu_ref[...].astype(BF16))
    o_ref[...] = _layer_norm(DEEPNORM_ALPHA * x + ple_gate * ple_up, lng_ref[1:2, :], lnb_ref[1:2, :])


def _ffn_ple(x2d, p2d, wgu, wd, wpu, wpg, bpg, lng, lnb):
    t, d = x2d.shape
    rows = FFN_ROWS
    return pl.pallas_call(
        _ffn_ple_kernel,
        grid=(t // rows,),
        in_specs=[
            pl.BlockSpec((rows, d), lambda i: (i, 0)),
            pl.BlockSpec((rows, PLE_DIM), lambda i: (i, 0)),
            _const_spec(wgu.shape), _const_spec(wd.shape), _const_spec(wpu.shape), _const_spec(wpg.shape),
            _const_spec(bpg.shape), _const_spec(lng.shape), _const_spec(lnb.shape),
        ],
        out_specs=pl.BlockSpec((rows, d), lambda i: (i, 0)),
        out_shape=jax.ShapeDtypeStruct(x2d.shape, F32),
        compiler_params=pltpu.CompilerParams(dimension_semantics=("arbitrary",),
                                             vmem_limit_bytes=VMEM_LIMIT_BYTES),
        name="ffn_ple",
    )(x2d, p2d, wgu, wd, wpu, wpg, bpg, lng, lnb)


def _swa_layer_kernel(slopes_ref, sinks_ref, x_ref, wkv_ref, bkv_ref, wq_ref, bq_ref, wo_ref, bo_ref,
                      lng_ref, lnb_ref, o_ref, kprev_ref, vprev_ref, att_ref, bias_ref):
    w = WINDOW
    rows = x_ref.shape[1]
    first = pl.program_id(1) == 0

    @pl.when((pl.program_id(0) == 0) & first)
    def _():
        qi = lax.broadcasted_iota(jnp.int32, (w, 2 * w), 0)
        si = lax.broadcasted_iota(jnp.int32, (w, 2 * w), 1)
        dist = qi - si + w
        band = (dist >= 0) & (dist < w)
        dist_f = dist.astype(F32)
        prev_keys = (si >= 1) & (si < w)
        for kh in range(ATT_KVH):
            for par in range(2):
                for r in range(2):
                    head = kh * ATT_G + 2 * r + par
                    bias = jnp.where(band, (slopes_ref[head] * dist_f) * (-LOG2E), -jnp.inf)
                    bias = jnp.where(si == 0, jnp.full((w, 2 * w), sinks_ref[head], F32) * LOG2E, bias)
                    bias_ref[0, kh, par, r * w:(r + 1) * w, :] = bias
                    bias_ref[1, kh, par, r * w:(r + 1) * w, :] = jnp.where(prev_keys, -jnp.inf, bias)

    @pl.when(first)
    def _():
        kprev_ref[...] = jnp.zeros_like(kprev_ref)
        vprev_ref[...] = jnp.zeros_like(vprev_ref)

    x = x_ref[0]
    xb = x.astype(BF16)
    kv = _dot(xb, wkv_ref[...].astype(BF16)) + bkv_ref[...]
    kd = kv[:, :ATT_KVH * LANES].astype(BF16)
    vd = kv[:, ATT_KVH * LANES:].astype(BF16)
    q = ((_dot(xb, wq_ref[...].astype(BF16)) + bq_ref[...]) * (ATT_HD ** -0.5 * LOG2E)).astype(BF16)

    lo_half = lax.broadcasted_iota(jnp.int32, (1, LANES), 1) < ATT_HD
    not_row0 = lax.broadcasted_iota(jnp.int32, (BF16_ROWS, 1), 0) > 0
    zero_bf = jnp.zeros((), BF16)
    first_idx = first.astype(jnp.int32)

    for j in range(rows // w):
        blk = slice(j * w, (j + 1) * w)
        if j == 0:
            k_prev, v_prev, tbl = kprev_ref[...], vprev_ref[...], first_idx
        else:
            prev = slice((j - 1) * w, j * w)
            k_prev, v_prev, tbl = kd[prev], vd[prev], 0
        keys = jnp.concatenate([jnp.where(not_row0, k_prev[:BF16_ROWS], zero_bf), k_prev[BF16_ROWS:], kd[blk]], axis=0)
        vals = jnp.concatenate([jnp.where(not_row0, v_prev[:BF16_ROWS], zero_bf), v_prev[BF16_ROWS:], vd[blk]], axis=0)
        for kh in range(ATT_KVH):
            grp = slice(kh * LANES, (kh + 1) * LANES)
            k_dup, v_dup = keys[:, grp], vals[:, grp]
            c0 = kh * ATT_G * ATT_HD
            q2 = jnp.concatenate([q[blk, c0:c0 + LANES], q[blk, c0 + LANES:c0 + 2 * LANES]], axis=0)
            out = None
            for par in range(2):
                keep = lo_half if par == 0 else jnp.logical_not(lo_half)
                s = _dot_nt(q2, jnp.where(keep, k_dup, zero_bf)) + bias_ref[tbl, kh, par]
                p = jnp.exp2(s - jnp.max(s, axis=-1, keepdims=True))
                inv = 1.0 / jnp.sum(p, axis=-1, keepdims=True)
                pv = _dot(p.astype(BF16), jnp.where(keep, v_dup, zero_bf)) * inv
                out = pv if out is None else out + pv
            att_ref[blk, c0:c0 + LANES] = out[:w].astype(BF16)
            att_ref[blk, c0 + LANES:c0 + 2 * LANES] = out[w:].astype(BF16)

    kprev_ref[...] = kd[rows - w:]
    vprev_ref[...] = vd[rows - w:]
    mix = _dot(att_ref[...], wo_ref[...].astype(BF16)) + bo_ref[...]
    o_ref[0] = _layer_norm(DEEPNORM_ALPHA * x + mix, lng_ref[...], lnb_ref[...])


def _swa_layer(x, slopes, sinks, wkv, bkv, wq, bq, wo, bo, lng, lnb):
    bsz, seq, d = x.shape
    rows = SWA_ROWS
    smem = pl.BlockSpec(memory_space=pltpu.SMEM)
    return pl.pallas_call(
        _swa_layer_kernel,
        grid=(bsz, seq // rows),
        in_specs=[
            smem, smem,
            pl.BlockSpec((1, rows, d), lambda b, i: (b, i, 0)),
            _const_spec(wkv.shape), _const_spec(bkv.shape), _const_spec(wq.shape), _const_spec(bq.shape),
            _const_spec(wo.shape), _const_spec(bo.shape), _const_spec(lng.shape), _const_spec(lnb.shape),
        ],
        out_specs=pl.BlockSpec((1, rows, d), lambda b, i: (b, i, 0)),
        out_shape=jax.ShapeDtypeStruct(x.shape, F32),
        scratch_shapes=[pltpu.VMEM((WINDOW, ATT_KVH * LANES), BF16), pltpu.VMEM((WINDOW, ATT_KVH * LANES), BF16),
                        pltpu.VMEM((rows, ATT_QH * ATT_HD), BF16),
                        pltpu.VMEM((2, ATT_KVH, 2, 2 * WINDOW, 2 * WINDOW), F32)],
        compiler_params=pltpu.CompilerParams(dimension_semantics=("arbitrary", "arbitrary"),
                                             vmem_limit_bytes=VMEM_LIMIT_BYTES),
        name="swa_layer",
    )(slopes, sinks, x, wkv, bkv, wq, bq, wo, bo, lng, lnb)


def _dup_heads(w):
    lead = w.shape[:-1]
    w4 = w.reshape(lead + (ATT_KVH, 1, ATT_HD))
    return jnp.broadcast_to(w4, lead + (ATT_KVH, 2, ATT_HD)).reshape(lead + (ATT_KVH * 2 * ATT_HD,))


def kernel(x, p, a_w_in, a_lower_bound, a_norm_gain, a_w_out, kv_w, kv_b, b_w_q, b_b_q, b_sinks, b_w_out, b_b_out,
           ffn_w_gate_up, ffn_w_down, ple_w_up, ple_w_gate, ple_b_gate, ln_gain, ln_bias):
    bsz, seq, d = x.shape
    row = lambda v: v.reshape(1, -1).astype(F32)
    lower_bounds = jnp.cumsum(jax.nn.softmax(a_lower_bound.astype(F32), axis=0), axis=0)

    def ffn_ple(i, h):
        out = _ffn_ple(h.reshape(bsz * seq, d), p[i].reshape(bsz * seq, PLE_DIM),
                       ffn_w_gate_up[i], ffn_w_down[i], ple_w_up[i],
                       ple_w_gate[i], row(ple_b_gate[i]), ln_gain[i, 1:3], ln_bias[i, 1:3])
        return out.reshape(bsz, seq, d)

    h = _hgrn_layer(x, a_w_in[0], row(lower_bounds[0]), row(a_norm_gain[0]), a_w_out[0],
                    row(ln_gain[0, 0]), row(ln_bias[0, 0]))
    h = ffn_ple(0, h)

    kdim = ATT_KVH * ATT_HD
    wkv = jnp.concatenate([_dup_heads(kv_w[:, :kdim]), _dup_heads(kv_w[:, kdim:])], axis=-1)
    bkv = row(jnp.concatenate([_dup_heads(kv_b[:kdim]), _dup_heads(kv_b[kdim:])], axis=-1))
    slopes = jnp.exp2(-8.0 * jnp.arange(1, ATT_QH + 1, dtype=F32) / ATT_QH)
    h = _swa_layer(h, slopes, b_sinks[0].astype(F32), wkv, bkv, b_w_q[0], row(b_b_q[0]),
                   b_w_out[0], row(b_b_out[0]), row(ln_gain[1, 0]), row(ln_bias[1, 0]))
    h = ffn_ple(1, h)
    return h
```

```python
import jax
import jax.numpy as jnp
from jax import lax
from jax.experimental import pallas as pl
from jax.experimental.pallas import tpu as pltpu

F32 = jnp.float32
BF16 = jnp.bfloat16

D_MODEL = 1024
DEPTH = 2
HG_DK = 128
HG_HEADS = 8
HG_WIDTH = HG_HEADS * HG_DK
ATT_HD = 64
ATT_QH = 16
ATT_KVH = 4
ATT_G = ATT_QH // ATT_KVH
WINDOW = 128
FFN_HIDDEN = 2816
PLE_DIM = 256
DEEPNORM_ALPHA = (2.0 * DEPTH) ** 0.25
LN_EPS = 1e-5
RMS_EPS = 1e-6

LANES = 128
SUBLANES = 8
BF16_ROWS = 16
LOG2E = 1.4426950408889634
VMEM_LIMIT_BYTES = 56 * 1024 * 1024

HG_CHUNK = 128
HG_ROWS = 256
HG_LEVELS = (64, 32, 16, 8, 4, 2, 1)
FFN_ROWS = 256
SWA_ROWS = 256


def _dot(a, b):
    return jnp.dot(a, b, preferred_element_type=F32)


def _dot_nt(a, b):
    return lax.dot_general(a, b, (((1,), (1,)), ((), ())), preferred_element_type=F32)


def _dot_tn(a, b):
    return lax.dot_general(a, b, (((0,), (0,)), ((), ())), preferred_element_type=F32)


def _sigmoid(x):
    return 1.0 / (1.0 + jnp.exp(-x))


def _layer_norm(x, g, b):
    mu = jnp.mean(x, axis=-1, keepdims=True)
    xc = x - mu
    var = jnp.mean(xc * xc, axis=-1, keepdims=True)
    return xc * lax.rsqrt(var + LN_EPS) * g + b


def _hgrn_level_exponent(n, b, lf, row, sign):
    c, width = b.shape
    if n >= 4:
        b3 = b.reshape(c // (2 * n), 2 * n, width)
        d = (b3 - b3[:, n - 1:n, :]).reshape(c, width)
        return d * sign
    if n == 2:
        p = row & 3
        nxt = pltpu.roll(lf, c - 1, axis=0)
        prv = pltpu.roll(lf, 1, axis=0)
        zero = jnp.zeros_like(lf)
        return (jnp.where(p == 0, nxt, zero) + jnp.where(p >= 2, lf, zero)
                + jnp.where(p == 3, prv, zero))
    return jnp.where((row & 1) == 1, lf, jnp.zeros_like(lf))


def _hgrn_level_operands(n, q_bf, k_bf, b, lf, row, signs):
    c = b.shape[0]
    if n < BF16_ROWS:
        e = jnp.exp2(_hgrn_level_exponent(n, b, lf, row, signs.get(n)).astype(BF16))
        return q_bf * e, k_bf * e
    q_parts, k_parts = [], []
    for g in range(c // (2 * n)):
        lo = slice(2 * n * g, 2 * n * g + n)
        up = slice(2 * n * g + n, 2 * n * (g + 1))
        ref_row = b[2 * n * g + n - 1:2 * n * g + n, :]
        q_parts.append(q_bf[up] * jnp.exp2((b[up] - ref_row).astype(BF16)))
        k_parts += [k_bf[lo] * jnp.exp2((ref_row - b[lo]).astype(BF16)), k_bf[up]]
    return jnp.concatenate(q_parts, axis=0), jnp.concatenate(k_parts, axis=0)


def _hgrn_merge_level(n, tiles, s_n, lvl_tiles):
    level = n.bit_length() - 1

    def merge(t, s_tile):
        old = jnp.zeros_like(s_tile) if tiles[t] is None else tiles[t]
        tiles[t] = jnp.where(lvl_tiles[t] == level, s_tile, old)

    if n >= BF16_ROWS:
        per = n // SUBLANES
        for i in range(s_n.shape[0] // SUBLANES):
            merge(2 * per * (i // per) + per + i % per, s_n[i * SUBLANES:(i + 1) * SUBLANES])
    else:
        for t in range(len(tiles)):
            if n == SUBLANES and t % 2 == 0:
                continue
            merge(t, s_n[t * SUBLANES:(t + 1) * SUBLANES])


def _hgrn_layer_kernel(x_ref, w_in_ref, lb_ref, gain_ref, w_out_ref, lng_ref, lnb_ref,
                       o_ref, st_ref, y_ref):
    c = HG_CHUNK

    @pl.when(pl.program_id(1) == 0)
    def _():
        st_ref[...] = jnp.zeros_like(st_ref)

    w_in = w_in_ref[...].astype(BF16)
    w_out = w_out_ref[...].astype(BF16)
    lb = lb_ref[...]
    one_m_lb = 1.0 - lb
    gain = gain_ref[...]

    rr = lax.broadcasted_iota(jnp.int32, (c, c), 0)
    cc = lax.broadcasted_iota(jnp.int32, (c, c), 1)
    tri = (rr >= cc).astype(BF16)
    xr = rr ^ cc
    lvl = jnp.full((c, c), -1, jnp.int32)
    for j in range(len(HG_LEVELS)):
        lvl = lvl + (xr >= (1 << j)).astype(jnp.int32)
    lvl = jnp.where(rr > cc, lvl, -1)
    lvl_tiles = [lvl[t * SUBLANES:(t + 1) * SUBLANES] for t in range(c // SUBLANES)]
    row = lax.broadcasted_iota(jnp.int32, (c, HG_WIDTH), 0)
    signs = {n: jnp.where(((row // n) & 1) == 1, 1.0, -1.0).astype(F32) for n in HG_LEVELS if 4 <= n < BF16_ROWS}
    heads = [slice(h * HG_DK, (h + 1) * HG_DK) for h in range(HG_HEADS)]

    def project(ci):
        x = x_ref[0, ci * c:(ci + 1) * c, :]
        return x, _dot(x.astype(BF16), w_in)

    def gates(proj):
        sig_f = _sigmoid(proj[:, HG_WIDTH:2 * HG_WIDTH])
        log_f = jnp.log2(lb + one_m_lb * sig_f)
        k_all = one_m_lb * (1.0 - sig_f)
        q_raw = proj[:, :HG_WIDTH]
        q_all = q_raw * _sigmoid(q_raw) * (HG_DK ** -0.5)
        v_all = proj[:, 2 * HG_WIDTH:3 * HG_WIDTH]
        g_raw = proj[:, 3 * HG_WIDTH:]
        g_all = g_raw * _sigmoid(g_raw)
        hi = log_f.astype(BF16)
        r1 = log_f - hi.astype(F32)
        mid = r1.astype(BF16)
        lo = (r1 - mid.astype(F32)).astype(BF16)
        b_all = _dot(tri, hi) + _dot(tri, mid) + _dot(tri, lo)
        return q_all, k_all, log_f, b_all, v_all, g_all

    def level_operands(vals):
        q_all, k_all, log_f, b_all, v_all, _ = vals
        q_bf, k_bf, v_bf = q_all.astype(BF16), k_all.astype(BF16), v_all.astype(BF16)
        operands = {n: _hgrn_level_operands(n, q_bf, k_bf, b_all, log_f, row, signs) for n in HG_LEVELS}
        b_last = b_all[c - 1:c, :]
        q_dec = q_bf * jnp.exp2(b_all.astype(BF16))
        k_dec = k_bf * jnp.exp2((b_last - b_all).astype(BF16))
        return operands, q_dec, k_dec, v_bf, jnp.exp2(b_last), q_all * k_all

    def scores_and_readout(vals, ops):
        v_all = vals[4]
        operands, q_dec, k_dec, v_bf, st_decay, qk = ops
        tiles = [[None] * (c // SUBLANES) for _ in heads]
        for n in HG_LEVELS:
            q_n, k_n = operands[n]
            for h, sl in enumerate(heads):
                _hgrn_merge_level(n, tiles[h], _dot_nt(q_n[:, sl], k_n[:, sl]), lvl_tiles)
        outs = []
        for h, sl in enumerate(heads):
            st = st_ref[h]
            o_h = _dot(jnp.concatenate(tiles[h], axis=0).astype(BF16), v_bf[:, sl])
            o_h = o_h + _dot_nt(q_dec[:, sl], st.astype(BF16))
            st_ref[h] = st * st_decay[:, sl] + _dot_tn(v_bf[:, sl], k_dec[:, sl])
            outs.append(o_h + jnp.sum(qk[:, sl], axis=-1, keepdims=True) * v_all[:, sl])
        return outs

    def finish(ci, x, g_all, outs):
        rows_c = slice(ci * c, (ci + 1) * c)
        for h, sl in enumerate(heads):
            ms = jnp.mean(outs[h] * outs[h], axis=-1, keepdims=True)
            y_ref[rows_c, sl] = (outs[h] * lax.rsqrt(ms + RMS_EPS) * gain * g_all[:, sl]).astype(BF16)
        mix = _dot(y_ref[rows_c, :], w_out)
        o_ref[0, rows_c, :] = _layer_norm(DEEPNORM_ALPHA * x + mix, lng_ref[...], lnb_ref[...])

    n_chunks = x_ref.shape[1] // c
    x_cur, proj = project(0)
    vals = gates(proj)
    for ci in range(n_chunks):
        ops = level_operands(vals)
        if ci + 1 < n_chunks:
            x_next, proj = project(ci + 1)
        outs = scores_and_readout(vals, ops)
        g_cur = vals[5]
        if ci + 1 < n_chunks:
            vals = gates(proj)
        finish(ci, x_cur, g_cur, outs)
        if ci + 1 < n_chunks:
            x_cur = x_next


def _const_spec(shape, layer=None):
    if layer is None:
        zeros = (0,) * len(shape)
        return pl.BlockSpec(shape, lambda *_: zeros, pipeline_mode=pl.Buffered(1))
    zeros = (0,) * (len(shape) - 1)
    return pl.BlockSpec((None,) + tuple(shape[1:]), lambda *_: (layer,) + zeros, pipeline_mode=pl.Buffered(1))


def _hgrn_layer(x, w_in, lb, gain, w_out, lng, lnb):
    bsz, seq, d = x.shape
    c = HG_ROWS
    return pl.pallas_call(
        _hgrn_layer_kernel,
        grid=(bsz, seq // c),
        in_specs=[
            pl.BlockSpec((1, c, d), lambda b, i: (b, i, 0)),
            _const_spec(w_in.shape, 0), _const_spec(lb.shape), _const_spec(gain.shape),
            _const_spec(w_out.shape, 0), _const_spec(lng.shape), _const_spec(lnb.shape),
        ],
        out_specs=pl.BlockSpec((1, c, d), lambda b, i: (b, i, 0)),
        out_shape=jax.ShapeDtypeStruct(x.shape, F32),
        scratch_shapes=[pltpu.VMEM((HG_HEADS, HG_DK, HG_DK), F32), pltpu.VMEM((c, HG_WIDTH), BF16)],
        compiler_params=pltpu.CompilerParams(dimension_semantics=("arbitrary", "arbitrary"),
                                             vmem_limit_bytes=VMEM_LIMIT_BYTES),
        name="hgrn_layer",
    )(x, w_in, lb, gain, w_out, lng, lnb)


def _ffn_ple_kernel(x_ref, p_ref, wgu_ref, wd_ref, wpu_ref, wpg_ref, bpg_ref, lng_ref, lnb_ref, o_ref):
    x = x_ref[...]
    gu = _dot(x.astype(BF16), wgu_ref[...].astype(BF16))
    gate, up = gu[:, :FFN_HIDDEN], gu[:, FFN_HIDDEN:]
    hid = (gate * _sigmoid(gate) * up).astype(BF16)
    x = _layer_norm(DEEPNORM_ALPHA * x + _dot(hid, wd_ref[...].astype(BF16)), lng_ref[0:1, :], lnb_ref[0:1, :])
    ple_gate = _sigmoid(_dot(x.astype(BF16), wpg_ref[...].astype(BF16)) + bpg_ref[...])
    ple_up = _dot(p_ref[...].astype(BF16), wpu_ref[...].astype(BF16))
    o_ref[...] = _layer_norm(DEEPNORM_ALPHA * x + ple_gate * ple_up, lng_ref[1:2, :], lnb_ref[1:2, :])


def _ffn_ple(layer, x2d, p3d, wgu, wd, wpu, wpg, bpg, lng, lnb):
    t, d = x2d.shape
    rows = FFN_ROWS
    return pl.pallas_call(
        _ffn_ple_kernel,
        grid=(t // rows,),
        in_specs=[
            pl.BlockSpec((rows, d), lambda i: (i, 0)),
            pl.BlockSpec((None, rows, PLE_DIM), lambda i: (layer, i, 0)),
            _const_spec(wgu.shape, layer), _const_spec(wd.shape, layer), _const_spec(wpu.shape, layer),
            _const_spec(wpg.shape, layer), _const_spec(bpg.shape), _const_spec(lng.shape), _const_spec(lnb.shape),
        ],
        out_specs=pl.BlockSpec((rows, d), lambda i: (i, 0)),
        out_shape=jax.ShapeDtypeStruct(x2d.shape, F32),
        compiler_params=pltpu.CompilerParams(dimension_semantics=("arbitrary",),
                                             vmem_limit_bytes=VMEM_LIMIT_BYTES),
        name="ffn_ple",
    )(x2d, p3d, wgu, wd, wpu, wpg, bpg, lng, lnb)


def _swa_layer_kernel(slopes_ref, sinks_ref, x_ref, wkv_ref, bkv_ref, wq_ref, bq_ref, wo_ref, bo_ref,
                      lng_ref, lnb_ref, o_ref, kprev_ref, vprev_ref, att_ref, bias_ref):
    w = WINDOW
    rows = x_ref.shape[1]
    first = pl.program_id(1) == 0

    @pl.when((pl.program_id(0) == 0) & first)
    def _():
        qi = lax.broadcasted_iota(jnp.int32, (w, 2 * w), 0)
        si = lax.broadcasted_iota(jnp.int32, (w, 2 * w), 1)
        dist = qi - si + w
        band = (dist >= 0) & (dist < w)
        dist_f = dist.astype(F32)
        prev_keys = (si >= 1) & (si < w)
        for kh in range(ATT_KVH):
            for par in range(2):
                for r in range(2):
                    head = kh * ATT_G + 2 * r + par
                    bias = jnp.where(band, (slopes_ref[head] * dist_f) * (-LOG2E), -jnp.inf)
                    bias = jnp.where(si == 0, jnp.full((w, 2 * w), sinks_ref[head], F32) * LOG2E, bias)
                    bias_ref[0, kh, par, r * w:(r + 1) * w, :] = bias
                    bias_ref[1, kh, par, r * w:(r + 1) * w, :] = jnp.where(prev_keys, -jnp.inf, bias)

    @pl.when(first)
    def _():
        kprev_ref[...] = jnp.zeros_like(kprev_ref)
        vprev_ref[...] = jnp.zeros_like(vprev_ref)

    x = x_ref[0]
    xb = x.astype(BF16)
    kv = _dot(xb, wkv_ref[...].astype(BF16)) + bkv_ref[...]
    kd = kv[:, :ATT_KVH * LANES].astype(BF16)
    vd = kv[:, ATT_KVH * LANES:].astype(BF16)
    q = ((_dot(xb, wq_ref[...].astype(BF16)) + bq_ref[...]) * (ATT_HD ** -0.5 * LOG2E)).astype(BF16)

    lo_half = lax.broadcasted_iota(jnp.int32, (1, LANES), 1) < ATT_HD
    not_row0 = lax.broadcasted_iota(jnp.int32, (BF16_ROWS, 1), 0) > 0
    zero_bf = jnp.zeros((), BF16)
    first_idx = first.astype(jnp.int32)

    for j in range(rows // w):
        blk = slice(j * w, (j + 1) * w)
        if j == 0:
            k_prev, v_prev, tbl = kprev_ref[...], vprev_ref[...], first_idx
        else:
            prev = slice((j - 1) * w, j * w)
            k_prev, v_prev, tbl = kd[prev], vd[prev], 0
        keys = jnp.concatenate([jnp.where(not_row0, k_prev[:BF16_ROWS], zero_bf), k_prev[BF16_ROWS:], kd[blk]], axis=0)
        vals = jnp.concatenate([jnp.where(not_row0, v_prev[:BF16_ROWS], zero_bf), v_prev[BF16_ROWS:], vd[blk]], axis=0)
        for kh in range(ATT_KVH):
            grp = slice(kh * LANES, (kh + 1) * LANES)
            k_dup, v_dup = keys[:, grp], vals[:, grp]
            c0 = kh * ATT_G * ATT_HD
            q2 = jnp.concatenate([q[blk, c0:c0 + LANES], q[blk, c0 + LANES:c0 + 2 * LANES]], axis=0)
            out = None
            for par in range(2):
                keep = lo_half if par == 0 else jnp.logical_not(lo_half)
                s = _dot_nt(q2, jnp.where(keep, k_dup, zero_bf)) + bias_ref[tbl, kh, par]
                p = jnp.exp2(s - jnp.max(s, axis=-1, keepdims=True))
                inv = 1.0 / jnp.sum(p, axis=-1, keepdims=True)
                pv = _dot(p.astype(BF16), jnp.where(keep, v_dup, zero_bf)) * inv
                out = pv if out is None else out + pv
            att_ref[blk, c0:c0 + LANES] = out[:w].astype(BF16)
            att_ref[blk, c0 + LANES:c0 + 2 * LANES] = out[w:].astype(BF16)

    kprev_ref[...] = kd[rows - w:]
    vprev_ref[...] = vd[rows - w:]
    mix = _dot(att_ref[...], wo_ref[...].astype(BF16)) + bo_ref[...]
    o_ref[0] = _layer_norm(DEEPNORM_ALPHA * x + mix, lng_ref[...], lnb_ref[...])


def _swa_layer(x, slopes, sinks, wkv, bkv, wq, bq, wo, bo, lng, lnb):
    bsz, seq, d = x.shape
    rows = SWA_ROWS
    smem = pl.BlockSpec(memory_space=pltpu.SMEM)
    return pl.pallas_call(
        _swa_layer_kernel,
        grid=(bsz, seq // rows),
        in_specs=[
            smem, smem,
            pl.BlockSpec((1, rows, d), lambda b, i: (b, i, 0)),
            _const_spec(wkv.shape), _const_spec(bkv.shape), _const_spec(wq.shape, 0), _const_spec(bq.shape),
            _const_spec(wo.shape, 0), _const_spec(bo.shape), _const_spec(lng.shape), _const_spec(lnb.shape),
        ],
        out_specs=pl.BlockSpec((1, rows, d), lambda b, i: (b, i, 0)),
        out_shape=jax.ShapeDtypeStruct(x.shape, F32),
        scratch_shapes=[pltpu.VMEM((WINDOW, ATT_KVH * LANES), BF16), pltpu.VMEM((WINDOW, ATT_KVH * LANES), BF16),
                        pltpu.VMEM((rows, ATT_QH * ATT_HD), BF16),
                        pltpu.VMEM((2, ATT_KVH, 2, 2 * WINDOW, 2 * WINDOW), F32)],
        compiler_params=pltpu.CompilerParams(dimension_semantics=("arbitrary", "arbitrary"),
                                             vmem_limit_bytes=VMEM_LIMIT_BYTES),
        name="swa_layer",
    )(slopes, sinks, x, wkv, bkv, wq, bq, wo, bo, lng, lnb)


def _dup_heads(w):
    lead = w.shape[:-1]
    w4 = w.reshape(lead + (ATT_KVH, 1, ATT_HD))
    return jnp.broadcast_to(w4, lead + (ATT_KVH, 2, ATT_HD)).reshape(lead + (ATT_KVH * 2 * ATT_HD,))


def kernel(x, p, a_w_in, a_lower_bound, a_norm_gain, a_w_out, kv_w, kv_b, b_w_q, b_b_q, b_sinks, b_w_out, b_b_out,
           ffn_w_gate_up, ffn_w_down, ple_w_up, ple_w_gate, ple_b_gate, ln_gain, ln_bias):
    bsz, seq, d = x.shape
    row = lambda v: v.reshape(1, -1).astype(F32)
    lower_bounds = jnp.cumsum(jax.nn.softmax(a_lower_bound.astype(F32), axis=0), axis=0)
    p3d = p.reshape(DEPTH, bsz * seq, PLE_DIM)

    def ffn_ple(i, h):
        out = _ffn_ple(i, h.reshape(bsz * seq, d), p3d, ffn_w_gate_up, ffn_w_down, ple_w_up, ple_w_gate,
                       row(ple_b_gate[i]), ln_gain[i, 1:3], ln_bias[i, 1:3])
        return out.reshape(bsz, seq, d)

    h = _hgrn_layer(x, a_w_in, row(lower_bounds[0]), row(a_norm_gain[0]), a_w_out,
                    row(ln_gain[0, 0]), row(ln_bias[0, 0]))
    h = ffn_ple(0, h)

    kdim = ATT_KVH * ATT_HD
    wkv = jnp.concatenate([_dup_heads(kv_w[:, :kdim]), _dup_heads(kv_w[:, kdim:])], axis=-1)
    bkv = row(jnp.concatenate([_dup_heads(kv_b[:kdim]), _dup_heads(kv_b[kdim:])], axis=-1))
    slopes = jnp.exp2(-8.0 * jnp.arange(1, ATT_QH + 1, dtype=F32) / ATT_QH)
    h = _swa_layer(h, slopes, b_sinks[0].astype(F32), wkv, bkv, b_w_q, row(b_b_q[0]),
                   b_w_out, row(b_b_out[0]), row(ln_gain[1, 0]), row(ln_bias[1, 0]))
    h = ffn_ple(1, h)
    return h
```

```python
import jax
import jax.numpy as jnp
from jax import lax
from jax.experimental import pallas as pl
from jax.experimental.pallas import tpu as pltpu

F32 = jnp.float32
BF16 = jnp.bfloat16

D_MODEL = 1024
DEPTH = 2
HG_DK = 128
HG_HEADS = 8
HG_WIDTH = HG_HEADS * HG_DK
ATT_HD = 64
ATT_QH = 16
ATT_KVH = 4
ATT_G = ATT_QH // ATT_KVH
WINDOW = 128
FFN_HIDDEN = 2816
PLE_DIM = 256
DEEPNORM_ALPHA = (2.0 * DEPTH) ** 0.25
LN_EPS = 1e-5
RMS_EPS = 1e-6

LANES = 128
SUBLANES = 8
BF16_ROWS = 16
LOG2E = 1.4426950408889634
VMEM_LIMIT_BYTES = 56 * 1024 * 1024

HG_CHUNK = 128
HG_ROWS = 512
HG_LEVELS = (64, 32, 16, 8, 4, 2, 1)
FFN_ROWS = 256
SWA_ROWS = 256


def _dot(a, b):
    return jnp.dot(a, b, preferred_element_type=F32)


def _dot_nt(a, b):
    return lax.dot_general(a, b, (((1,), (1,)), ((), ())), preferred_element_type=F32)


def _dot_tn(a, b):
    return lax.dot_general(a, b, (((0,), (0,)), ((), ())), preferred_element_type=F32)


def _sigmoid(x):
    return 1.0 / (1.0 + jnp.exp(-x))


def _layer_norm(x, g, b):
    mu = jnp.mean(x, axis=-1, keepdims=True)
    xc = x - mu
    var = jnp.mean(xc * xc, axis=-1, keepdims=True)
    return xc * lax.rsqrt(var + LN_EPS) * g + b


def _hgrn_level_exponent(n, b, lf, row, sign):
    c, width = b.shape
    if n >= 4:
        b3 = b.reshape(c // (2 * n), 2 * n, width)
        d = (b3 - b3[:, n - 1:n, :]).reshape(c, width)
        return d * sign
    if n == 2:
        p = row & 3
        nxt = pltpu.roll(lf, c - 1, axis=0)
        prv = pltpu.roll(lf, 1, axis=0)
        zero = jnp.zeros_like(lf)
        return (jnp.where(p == 0, nxt, zero) + jnp.where(p >= 2, lf, zero)
                + jnp.where(p == 3, prv, zero))
    return jnp.where((row & 1) == 1, lf, jnp.zeros_like(lf))


def _hgrn_level_operands(n, q_bf, k_bf, b, lf, row, signs):
    c = b.shape[0]
    if n < BF16_ROWS:
        e = jnp.exp2(_hgrn_level_exponent(n, b, lf, row, signs.get(n)).astype(BF16))
        return q_bf * e, k_bf * e
    q_parts, k_parts = [], []
    for g in range(c // (2 * n)):
        lo = slice(2 * n * g, 2 * n * g + n)
        up = slice(2 * n * g + n, 2 * n * (g + 1))
        ref_row = b[2 * n * g + n - 1:2 * n * g + n, :]
        q_parts.append(q_bf[up] * jnp.exp2((b[up] - ref_row).astype(BF16)))
        k_parts += [k_bf[lo] * jnp.exp2((ref_row - b[lo]).astype(BF16)), k_bf[up]]
    return jnp.concatenate(q_parts, axis=0), jnp.concatenate(k_parts, axis=0)


def _hgrn_merge_level(n, tiles, s_n, lvl_tiles):
    level = n.bit_length() - 1

    def merge(t, s_tile):
        old = jnp.zeros_like(s_tile) if tiles[t] is None else tiles[t]
        tiles[t] = jnp.where(lvl_tiles[t] == level, s_tile, old)

    if n >= BF16_ROWS:
        per = n // SUBLANES
        for i in range(s_n.shape[0] // SUBLANES):
            merge(2 * per * (i // per) + per + i % per, s_n[i * SUBLANES:(i + 1) * SUBLANES])
    else:
        for t in range(len(tiles)):
            if n == SUBLANES and t % 2 == 0:
                continue
            merge(t, s_n[t * SUBLANES:(t + 1) * SUBLANES])


def _hgrn_layer_kernel(x_ref, w_in_ref, lb_ref, gain_ref, w_out_ref, lng_ref, lnb_ref,
                       o_ref, st_ref, y_ref):
    c = HG_CHUNK

    @pl.when(pl.program_id(1) == 0)
    def _():
        st_ref[...] = jnp.zeros_like(st_ref)

    w_in = w_in_ref[...].astype(BF16)
    w_out = w_out_ref[...].astype(BF16)
    lb = lb_ref[...]
    one_m_lb = 1.0 - lb
    gain = gain_ref[...]

    rr = lax.broadcasted_iota(jnp.int32, (c, c), 0)
    cc = lax.broadcasted_iota(jnp.int32, (c, c), 1)
    tri = (rr >= cc).astype(BF16)
    xr = rr ^ cc
    lvl = jnp.full((c, c), -1, jnp.int32)
    for j in range(len(HG_LEVELS)):
        lvl = lvl + (xr >= (1 << j)).astype(jnp.int32)
    lvl = jnp.where(rr > cc, lvl, -1)
    lvl_tiles = [lvl[t * SUBLANES:(t + 1) * SUBLANES] for t in range(c // SUBLANES)]
    row = lax.broadcasted_iota(jnp.int32, (c, HG_WIDTH), 0)
    signs = {n: jnp.where(((row // n) & 1) == 1, 1.0, -1.0).astype(F32) for n in HG_LEVELS if 4 <= n < BF16_ROWS}
    heads = [slice(h * HG_DK, (h + 1) * HG_DK) for h in range(HG_HEADS)]

    def project(ci, after=None):
        x = x_ref[0, ci * c:(ci + 1) * c, :]
        lhs = x
        if after is not None:
            dep = after[c - SUBLANES:, :LANES]
            corner = jnp.where(dep > jnp.inf, dep, x[:SUBLANES, :LANES])
            lhs = jnp.concatenate([jnp.concatenate([corner, x[:SUBLANES, LANES:]], axis=1), x[SUBLANES:]], axis=0)
        return x, _dot(lhs.astype(BF16), w_in)

    def gates(proj):
        sig_f = _sigmoid(proj[:, HG_WIDTH:2 * HG_WIDTH])
        log_f = jnp.log2(lb + one_m_lb * sig_f)
        k_all = one_m_lb * (1.0 - sig_f)
        q_raw = proj[:, :HG_WIDTH]
        q_all = q_raw * _sigmoid(q_raw) * (HG_DK ** -0.5)
        v_all = proj[:, 2 * HG_WIDTH:3 * HG_WIDTH]
        g_raw = proj[:, 3 * HG_WIDTH:]
        g_all = g_raw * _sigmoid(g_raw)
        hi = log_f.astype(BF16)
        r1 = log_f - hi.astype(F32)
        mid = r1.astype(BF16)
        lo = (r1 - mid.astype(F32)).astype(BF16)
        b_all = _dot(tri, hi) + _dot(tri, mid) + _dot(tri, lo)
        return q_all, k_all, log_f, b_all, v_all, g_all

    def level_operands(vals):
        q_all, k_all, log_f, b_all, v_all, _ = vals
        q_bf, k_bf, v_bf = q_all.astype(BF16), k_all.astype(BF16), v_all.astype(BF16)
        operands = {n: _hgrn_level_operands(n, q_bf, k_bf, b_all, log_f, row, signs) for n in HG_LEVELS}
        b_last = b_all[c - 1:c, :]
        q_dec = q_bf * jnp.exp2(b_all.astype(BF16))
        k_dec = k_bf * jnp.exp2((b_last - b_all).astype(BF16))
        return operands, q_dec, k_dec, v_bf, jnp.exp2(b_last), q_all * k_all

    def scores_and_readout(vals, ops):
        v_all = vals[4]
        operands, q_dec, k_dec, v_bf, st_decay, qk = ops
        tiles = [[None] * (c // SUBLANES) for _ in heads]
        for n in HG_LEVELS:
            q_n, k_n = operands[n]
            for h, sl in enumerate(heads):
                _hgrn_merge_level(n, tiles[h], _dot_nt(q_n[:, sl], k_n[:, sl]), lvl_tiles)
        outs = []
        for h, sl in enumerate(heads):
            st = st_ref[h]
            o_h = _dot(jnp.concatenate(tiles[h], axis=0).astype(BF16), v_bf[:, sl])
            o_h = o_h + _dot_nt(q_dec[:, sl], st.astype(BF16))
            st_ref[h] = st * st_decay[:, sl] + _dot_tn(v_bf[:, sl], k_dec[:, sl])
            outs.append(o_h + jnp.sum(qk[:, sl], axis=-1, keepdims=True) * v_all[:, sl])
        return outs

    def finish(ci, x, g_all, outs):
        rows_c = slice(ci * c, (ci + 1) * c)
        for h, sl in enumerate(heads):
            ms = jnp.mean(outs[h] * outs[h], axis=-1, keepdims=True)
            y_ref[rows_c, sl] = (outs[h] * lax.rsqrt(ms + RMS_EPS) * gain * g_all[:, sl]).astype(BF16)
        mix = _dot(y_ref[rows_c, :], w_out)
        o_ref[0, rows_c, :] = _layer_norm(DEEPNORM_ALPHA * x + mix, lng_ref[...], lnb_ref[...])

    n_chunks = x_ref.shape[1] // c
    x_cur, proj = project(0)
    vals = gates(proj)
    for ci in range(n_chunks):
        ops = level_operands(vals)
        if ci + 1 < n_chunks:
            x_next, proj = project(ci + 1, after=vals[3])
        outs = scores_and_readout(vals, ops)
        g_cur = vals[5]
        if ci + 1 < n_chunks:
            vals = gates(proj)
        finish(ci, x_cur, g_cur, outs)
        if ci + 1 < n_chunks:
            x_cur = x_next


def _const_spec(shape, layer=None):
    if layer is None:
        zeros = (0,) * len(shape)
        return pl.BlockSpec(shape, lambda *_: zeros, pipeline_mode=pl.Buffered(1))
    zeros = (0,) * (len(shape) - 1)
    return pl.BlockSpec((None,) + tuple(shape[1:]), lambda *_: (layer,) + zeros, pipeline_mode=pl.Buffered(1))


def _hgrn_layer(x, w_in, lb, gain, w_out, lng, lnb):
    bsz, seq, d = x.shape
    c = HG_ROWS
    return pl.pallas_call(
        _hgrn_layer_kernel,
        grid=(bsz, seq // c),
        in_specs=[
            pl.BlockSpec((1, c, d), lambda b, i: (b, i, 0)),
            _const_spec(w_in.shape, 0), _const_spec(lb.shape), _const_spec(gain.shape),
            _const_spec(w_out.shape, 0), _const_spec(lng.shape), _const_spec(lnb.shape),
        ],
        out_specs=pl.BlockSpec((1, c, d), lambda b, i: (b, i, 0)),
        out_shape=jax.ShapeDtypeStruct(x.shape, F32),
        scratch_shapes=[pltpu.VMEM((HG_HEADS, HG_DK, HG_DK), F32), pltpu.VMEM((c, HG_WIDTH), BF16)],
        compiler_params=pltpu.CompilerParams(dimension_semantics=("arbitrary", "arbitrary"),
                                             vmem_limit_bytes=VMEM_LIMIT_BYTES),
        name="hgrn_layer",
    )(x, w_in, lb, gain, w_out, lng, lnb)


def _ffn_ple_kernel(x_ref, p_ref, wgu_ref, wd_ref, wpu_ref, wpg_ref, bpg_ref, lng_ref, lnb_ref, o_ref):
    x = x_ref[...]
    gu = _dot(x.astype(BF16), wgu_ref[...].astype(BF16))
    gate, up = gu[:, :FFN_HIDDEN], gu[:, FFN_HIDDEN:]
    hid = (gate * _sigmoid(gate) * up).astype(BF16)
    x = _layer_norm(DEEPNORM_ALPHA * x + _dot(hid, wd_ref[...].astype(BF16)), lng_ref[0:1, :], lnb_ref[0:1, :])
    ple_gate = _sigmoid(_dot(x.astype(BF16), wpg_ref[...].astype(BF16)) + bpg_ref[...])
    ple_up = _dot(p_ref[...].astype(BF16), wpu_ref[...].astype(BF16))
    o_ref[...] = _layer_norm(DEEPNORM_ALPHA * x + ple_gate * ple_up, lng_ref[1:2, :], lnb_ref[1:2, :])


def _ffn_ple(layer, x2d, p3d, wgu, wd, wpu, wpg, bpg, lng, lnb):
    t, d = x2d.shape
    rows = FFN_ROWS
    return pl.pallas_call(
        _ffn_ple_kernel,
        grid=(t // rows,),
        in_specs=[
            pl.BlockSpec((rows, d), lambda i: (i, 0)),
            pl.BlockSpec((None, rows, PLE_DIM), lambda i: (layer, i, 0)),
            _const_spec(wgu.shape, layer), _const_spec(wd.shape, layer), _const_spec(wpu.shape, layer),
            _const_spec(wpg.shape, layer), _const_spec(bpg.shape), _const_spec(lng.shape), _const_spec(lnb.shape),
        ],
        out_specs=pl.BlockSpec((rows, d), lambda i: (i, 0)),
        out_shape=jax.ShapeDtypeStruct(x2d.shape, F32),
        compiler_params=pltpu.CompilerParams(dimension_semantics=("arbitrary",),
                                             vmem_limit_bytes=VMEM_LIMIT_BYTES),
        name="ffn_ple",
    )(x2d, p3d, wgu, wd, wpu, wpg, bpg, lng, lnb)


def _swa_layer_kernel(slopes_ref, sinks_ref, x_ref, wkv_ref, bkv_ref, wq_ref, bq_ref, wo_ref, bo_ref,
                      lng_ref, lnb_ref, o_ref, kprev_ref, vprev_ref, att_ref, bias_ref):
    w = WINDOW
    rows = x_ref.shape[1]
    first = pl.program_id(1) == 0

    @pl.when((pl.program_id(0) == 0) & first)
    def _():
        qi = lax.broadcasted_iota(jnp.int32, (w, 2 * w), 0)
        si = lax.broadcasted_iota(jnp.int32, (w, 2 * w), 1)
        dist = qi - si + w
        band = (dist >= 0) & (dist < w)
        dist_f = dist.astype(F32)
        prev_keys = (si >= 1) & (si < w)
        for kh in range(ATT_KVH):
            for par in range(2):
                for r in range(2):
                    head = kh * ATT_G + 2 * r + par
                    bias = jnp.where(band, (slopes_ref[head] * dist_f) * (-LOG2E), -jnp.inf)
                    bias = jnp.where(si == 0, jnp.full((w, 2 * w), sinks_ref[head], F32) * LOG2E, bias)
                    bias_ref[0, kh, par, r * w:(r + 1) * w, :] = bias
                    bias_ref[1, kh, par, r * w:(r + 1) * w, :] = jnp.where(prev_keys, -jnp.inf, bias)

    @pl.when(first)
    def _():
        kprev_ref[...] = jnp.zeros_like(kprev_ref)
        vprev_ref[...] = jnp.zeros_like(vprev_ref)

    x = x_ref[0]
    xb = x.astype(BF16)
    kv = _dot(xb, wkv_ref[...].astype(BF16)) + bkv_ref[...]
    kd = kv[:, :ATT_KVH * LANES].astype(BF16)
    vd = kv[:, ATT_KVH * LANES:].astype(BF16)
    q = ((_dot(xb, wq_ref[...].astype(BF16)) + bq_ref[...]) * (ATT_HD ** -0.5 * LOG2E)).astype(BF16)

    lo_half = lax.broadcasted_iota(jnp.int32, (1, LANES), 1) < ATT_HD
    not_row0 = lax.broadcasted_iota(jnp.int32, (BF16_ROWS, 1), 0) > 0
    zero_bf = jnp.zeros((), BF16)
    first_idx = first.astype(jnp.int32)

    for j in range(rows // w):
        blk = slice(j * w, (j + 1) * w)
        if j == 0:
            k_prev, v_prev, tbl = kprev_ref[...], vprev_ref[...], first_idx
        else:
            prev = slice((j - 1) * w, j * w)
            k_prev, v_prev, tbl = kd[prev], vd[prev], 0
        keys = jnp.concatenate([jnp.where(not_row0, k_prev[:BF16_ROWS], zero_bf), k_prev[BF16_ROWS:], kd[blk]], axis=0)
        vals = jnp.concatenate([jnp.where(not_row0, v_prev[:BF16_ROWS], zero_bf), v_prev[BF16_ROWS:], vd[blk]], axis=0)
        for kh in range(ATT_KVH):
            grp = slice(kh * LANES, (kh + 1) * LANES)
            k_dup, v_dup = keys[:, grp], vals[:, grp]
            c0 = kh * ATT_G * ATT_HD
            q2 = jnp.concatenate([q[blk, c0:c0 + LANES], q[blk, c0 + LANES:c0 + 2 * LANES]], axis=0)
            out = None
            for par in range(2):
                keep = lo_half if par == 0 else jnp.logical_not(lo_half)
                s = _dot_nt(q2, jnp.where(keep, k_dup, zero_bf)) + bias_ref[tbl, kh, par]
                p = jnp.exp2(s - jnp.max(s, axis=-1, keepdims=True))
                inv = 1.0 / jnp.sum(p, axis=-1, keepdims=True)
                pv = _dot(p.astype(BF16), jnp.where(keep, v_dup, zero_bf)) * inv
                out = pv if out is None else out + pv
            att_ref[blk, c0:c0 + LANES] = out[:w].astype(BF16)
            att_ref[blk, c0 + LANES:c0 + 2 * LANES] = out[w:].astype(BF16)

    kprev_ref[...] = kd[rows - w:]
    vprev_ref[...] = vd[rows - w:]
    mix = _dot(att_ref[...], wo_ref[...].astype(BF16)) + bo_ref[...]
    o_ref[0] = _layer_norm(DEEPNORM_ALPHA * x + mix, lng_ref[...], lnb_ref[...])


def _swa_layer(x, slopes, sinks, wkv, bkv, wq, bq, wo, bo, lng, lnb):
    bsz, seq, d = x.shape
    rows = SWA_ROWS
    smem = pl.BlockSpec(memory_space=pltpu.SMEM)
    return pl.pallas_call(
        _swa_layer_kernel,
        grid=(bsz, seq // rows),
        in_specs=[
            smem, smem,
            pl.BlockSpec((1, rows, d), lambda b, i: (b, i, 0)),
            _const_spec(wkv.shape), _const_spec(bkv.shape), _const_spec(wq.shape, 0), _const_spec(bq.shape),
            _const_spec(wo.shape, 0), _const_spec(bo.shape), _const_spec(lng.shape), _const_spec(lnb.shape),
        ],
        out_specs=pl.BlockSpec((1, rows, d), lambda b, i: (b, i, 0)),
        out_shape=jax.ShapeDtypeStruct(x.shape, F32),
        scratch_shapes=[pltpu.VMEM((WINDOW, ATT_KVH * LANES), BF16), pltpu.VMEM((WINDOW, ATT_KVH * LANES), BF16),
                        pltpu.VMEM((rows, ATT_QH * ATT_HD), BF16),
                        pltpu.VMEM((2, ATT_KVH, 2, 2 * WINDOW, 2 * WINDOW), F32)],
        compiler_params=pltpu.CompilerParams(dimension_semantics=("arbitrary", "arbitrary"),
                                             vmem_limit_bytes=VMEM_LIMIT_BYTES),
        name="swa_layer",
    )(slopes, sinks, x, wkv, bkv, wq, bq, wo, bo, lng, lnb)


def _dup_heads(w):
    lead = w.shape[:-1]
    w4 = w.reshape(lead + (ATT_KVH, 1, ATT_HD))
    return jnp.broadcast_to(w4, lead + (ATT_KVH, 2, ATT_HD)).reshape(lead + (ATT_KVH * 2 * ATT_HD,))


def kernel(x, p, a_w_in, a_lower_bound, a_norm_gain, a_w_out, kv_w, kv_b, b_w_q, b_b_q, b_sinks, b_w_out, b_b_out,
           ffn_w_gate_up, ffn_w_down, ple_w_up, ple_w_gate, ple_b_gate, ln_gain, ln_bias):
    bsz, seq, d = x.shape
    row = lambda v: v.reshape(1, -1).astype(F32)
    lower_bounds = jnp.cumsum(jax.nn.softmax(a_lower_bound.astype(F32), axis=0), axis=0)
    p3d = p.reshape(DEPTH, bsz * seq, PLE_DIM)

    def ffn_ple(i, h):
        out = _ffn_ple(i, h.reshape(bsz * seq, d), p3d, ffn_w_gate_up, ffn_w_down, ple_w_up, ple_w_gate,
                       row(ple_b_gate[i]), ln_gain[i, 1:3], ln_bias[i, 1:3])
        return out.reshape(bsz, seq, d)

    h = _hgrn_layer(x, a_w_in, row(lower_bounds[0]), row(a_norm_gain[0]), a_w_out,
                    row(ln_gain[0, 0]), row(ln_bias[0, 0]))
    h = ffn_ple(0, h)

    kdim = ATT_KVH * ATT_HD
    wkv = jnp.concatenate([_dup_heads(kv_w[:, :kdim]), _dup_heads(kv_w[:, kdim:])], axis=-1)
    bkv = row(jnp.concatenate([_dup_heads(kv_b[:kdim]), _dup_heads(kv_b[kdim:])], axis=-1))
    slopes = jnp.exp2(-8.0 * jnp.arange(1, ATT_QH + 1, dtype=F32) / ATT_QH)
    h = _swa_layer(h, slopes, b_sinks[0].astype(F32), wkv, bkv, b_w_q, row(b_b_q[0]),
                   b_w_out, row(b_b_out[0]), row(ln_gain[1, 0]), row(ln_bias[1, 0]))
    h = ffn_ple(1, h)
    return h
```

```python
import jax
import jax.numpy as jnp
from jax import lax
from jax.experimental import pallas as pl
from jax.experimental.pallas import tpu as pltpu

F32 = jnp.float32
BF16 = jnp.bfloat16

D_MODEL = 1024
DEPTH = 2
HG_DK = 128
HG_HEADS = 8
HG_WIDTH = HG_HEADS * HG_DK
ATT_HD = 64
ATT_QH = 16
ATT_KVH = 4
ATT_G = ATT_QH // ATT_KVH
WINDOW = 128
FFN_HIDDEN = 2816
PLE_DIM = 256
DEEPNORM_ALPHA = (2.0 * DEPTH) ** 0.25
LN_EPS = 1e-5
RMS_EPS = 1e-6

LANES = 128
SUBLANES = 8
BF16_ROWS = 16
LOG2E = 1.4426950408889634
VMEM_LIMIT_BYTES = 56 * 1024 * 1024

HG_CHUNK = 128
HG_ROWS = 512
HG_LEVELS = (64, 32, 16, 8, 4, 2, 1)
FFN_ROWS = 512
FFN_SUB_ROWS = 256
SWA_ROWS = 256


def _dot(a, b):
    return jnp.dot(a, b, preferred_element_type=F32)


def _dot_nt(a, b):
    return lax.dot_general(a, b, (((1,), (1,)), ((), ())), preferred_element_type=F32)


def _dot_tn(a, b):
    return lax.dot_general(a, b, (((0,), (0,)), ((), ())), preferred_element_type=F32)


def _sigmoid(x):
    return 1.0 / (1.0 + jnp.exp(-x))


def _layer_norm(x, g, b):
    mu = jnp.mean(x, axis=-1, keepdims=True)
    xc = x - mu
    var = jnp.mean(xc * xc, axis=-1, keepdims=True)
    return xc * lax.rsqrt(var + LN_EPS) * g + b


def _hgrn_level_exponent(n, b, lf, row, sign):
    c, width = b.shape
    if n >= 4:
        b3 = b.reshape(c // (2 * n), 2 * n, width)
        d = (b3 - b3[:, n - 1:n, :]).reshape(c, width)
        return d * sign
    if n == 2:
        p = row & 3
        nxt = pltpu.roll(lf, c - 1, axis=0)
        prv = pltpu.roll(lf, 1, axis=0)
        zero = jnp.zeros_like(lf)
        return (jnp.where(p == 0, nxt, zero) + jnp.where(p >= 2, lf, zero)
                + jnp.where(p == 3, prv, zero))
    return jnp.where((row & 1) == 1, lf, jnp.zeros_like(lf))


def _hgrn_level_operands(n, q_bf, k_bf, b, lf, row, signs):
    c = b.shape[0]
    if n < BF16_ROWS:
        e = jnp.exp2(_hgrn_level_exponent(n, b, lf, row, signs.get(n)).astype(BF16))
        return q_bf * e, k_bf * e
    q_parts, k_parts = [], []
    for g in range(c // (2 * n)):
        lo = slice(2 * n * g, 2 * n * g + n)
        up = slice(2 * n * g + n, 2 * n * (g + 1))
        ref_row = b[2 * n * g + n - 1:2 * n * g + n, :]
        q_parts.append(q_bf[up] * jnp.exp2((b[up] - ref_row).astype(BF16)))
        k_parts += [k_bf[lo] * jnp.exp2((ref_row - b[lo]).astype(BF16)), k_bf[up]]
    return jnp.concatenate(q_parts, axis=0), jnp.concatenate(k_parts, axis=0)


def _hgrn_merge_level(n, tiles, s_n, lvl_tiles):
    level = n.bit_length() - 1

    def merge(t, s_tile):
        old = jnp.zeros_like(s_tile) if tiles[t] is None else tiles[t]
        tiles[t] = jnp.where(lvl_tiles[t] == level, s_tile, old)

    if n >= BF16_ROWS:
        per = n // SUBLANES
        for i in range(s_n.shape[0] // SUBLANES):
            merge(2 * per * (i // per) + per + i % per, s_n[i * SUBLANES:(i + 1) * SUBLANES])
    else:
        for t in range(len(tiles)):
            if n == SUBLANES and t % 2 == 0:
                continue
            merge(t, s_n[t * SUBLANES:(t + 1) * SUBLANES])


def _hgrn_layer_kernel(x_ref, w_in_ref, lb_ref, gain_ref, w_out_ref, lng_ref, lnb_ref,
                       o_ref, st_ref, y_ref):
    c = HG_CHUNK

    @pl.when(pl.program_id(1) == 0)
    def _():
        st_ref[...] = jnp.zeros_like(st_ref)

    w_in = w_in_ref[...].astype(BF16)
    w_out = w_out_ref[...].astype(BF16)
    lb = lb_ref[...]
    one_m_lb = 1.0 - lb
    gain = gain_ref[...]

    rr = lax.broadcasted_iota(jnp.int32, (c, c), 0)
    cc = lax.broadcasted_iota(jnp.int32, (c, c), 1)
    tri = (rr >= cc).astype(BF16)
    xr = rr ^ cc
    lvl = jnp.full((c, c), -1, jnp.int32)
    for j in range(len(HG_LEVELS)):
        lvl = lvl + (xr >= (1 << j)).astype(jnp.int32)
    lvl = jnp.where(rr > cc, lvl, -1)
    lvl_tiles = [lvl[t * SUBLANES:(t + 1) * SUBLANES] for t in range(c // SUBLANES)]
    row = lax.broadcasted_iota(jnp.int32, (c, HG_WIDTH), 0)
    signs = {n: jnp.where(((row // n) & 1) == 1, 1.0, -1.0).astype(F32) for n in HG_LEVELS if 4 <= n < BF16_ROWS}
    heads = [slice(h * HG_DK, (h + 1) * HG_DK) for h in range(HG_HEADS)]

    def project(ci, after=None):
        x = x_ref[0, ci * c:(ci + 1) * c, :]
        lhs = x
        if after is not None:
            dep = after[c - SUBLANES:, :LANES]
            corner = jnp.where(dep > jnp.inf, dep, x[:SUBLANES, :LANES])
            lhs = jnp.concatenate([jnp.concatenate([corner, x[:SUBLANES, LANES:]], axis=1), x[SUBLANES:]], axis=0)
        return x, _dot(lhs.astype(BF16), w_in)

    def gates(proj):
        sig_f = _sigmoid(proj[:, HG_WIDTH:2 * HG_WIDTH])
        log_f = jnp.log2(lb + one_m_lb * sig_f)
        k_all = one_m_lb * (1.0 - sig_f)
        q_raw = proj[:, :HG_WIDTH]
        q_all = q_raw * _sigmoid(q_raw) * (HG_DK ** -0.5)
        v_all = proj[:, 2 * HG_WIDTH:3 * HG_WIDTH]
        g_raw = proj[:, 3 * HG_WIDTH:]
        g_all = g_raw * _sigmoid(g_raw)
        hi = log_f.astype(BF16)
        r1 = log_f - hi.astype(F32)
        mid = r1.astype(BF16)
        lo = (r1 - mid.astype(F32)).astype(BF16)
        b_all = _dot(tri, hi) + _dot(tri, mid) + _dot(tri, lo)
        return q_all, k_all, log_f, b_all, v_all, g_all

    def level_operands(vals):
        q_all, k_all, log_f, b_all, v_all, _ = vals
        q_bf, k_bf, v_bf = q_all.astype(BF16), k_all.astype(BF16), v_all.astype(BF16)
        operands = {n: _hgrn_level_operands(n, q_bf, k_bf, b_all, log_f, row, signs) for n in HG_LEVELS}
        b_last = b_all[c - 1:c, :]
        q_dec = q_bf * jnp.exp2(b_all.astype(BF16))
        k_dec = k_bf * jnp.exp2((b_last - b_all).astype(BF16))
        return operands, q_dec, k_dec, v_bf, jnp.exp2(b_last), q_all * k_all

    def scores_and_readout(vals, ops):
        v_all = vals[4]
        operands, q_dec, k_dec, v_bf, st_decay, qk = ops
        tiles = [[None] * (c // SUBLANES) for _ in heads]
        for n in HG_LEVELS:
            q_n, k_n = operands[n]
            for h, sl in enumerate(heads):
                _hgrn_merge_level(n, tiles[h], _dot_nt(q_n[:, sl], k_n[:, sl]), lvl_tiles)
        outs = []
        for h, sl in enumerate(heads):
            st = st_ref[h]
            o_h = _dot(jnp.concatenate(tiles[h], axis=0).astype(BF16), v_bf[:, sl])
            o_h = o_h + _dot_nt(q_dec[:, sl], st.astype(BF16))
            st_ref[h] = st * st_decay[:, sl] + _dot_tn(v_bf[:, sl], k_dec[:, sl])
            outs.append(o_h + jnp.sum(qk[:, sl], axis=-1, keepdims=True) * v_all[:, sl])
        return outs

    def finish(ci, x, g_all, outs):
        rows_c = slice(ci * c, (ci + 1) * c)
        for h, sl in enumerate(heads):
            ms = jnp.mean(outs[h] * outs[h], axis=-1, keepdims=True)
            y_ref[rows_c, sl] = (outs[h] * lax.rsqrt(ms + RMS_EPS) * gain * g_all[:, sl]).astype(BF16)
        mix = _dot(y_ref[rows_c, :], w_out)
        o_ref[0, rows_c, :] = _layer_norm(DEEPNORM_ALPHA * x + mix, lng_ref[...], lnb_ref[...])

    n_chunks = x_ref.shape[1] // c
    x_cur, proj = project(0)
    vals = gates(proj)
    for ci in range(n_chunks):
        ops = level_operands(vals)
        if ci + 1 < n_chunks:
            x_next, proj = project(ci + 1, after=vals[3])
        outs = scores_and_readout(vals, ops)
        g_cur = vals[5]
        if ci + 1 < n_chunks:
            vals = gates(proj)
        finish(ci, x_cur, g_cur, outs)
        if ci + 1 < n_chunks:
            x_cur = x_next


def _const_spec(shape, layer=None):
    if layer is None:
        zeros = (0,) * len(shape)
        return pl.BlockSpec(shape, lambda *_: zeros, pipeline_mode=pl.Buffered(1))
    zeros = (0,) * (len(shape) - 1)
    return pl.BlockSpec((None,) + tuple(shape[1:]), lambda *_: (layer,) + zeros, pipeline_mode=pl.Buffered(1))


def _hgrn_layer(x, w_in, lb, gain, w_out, lng, lnb):
    bsz, seq, d = x.shape
    c = HG_ROWS
    return pl.pallas_call(
        _hgrn_layer_kernel,
        grid=(bsz, seq // c),
        in_specs=[
            pl.BlockSpec((1, c, d), lambda b, i: (b, i, 0)),
            _const_spec(w_in.shape, 0), _const_spec(lb.shape), _const_spec(gain.shape),
            _const_spec(w_out.shape, 0), _const_spec(lng.shape), _const_spec(lnb.shape),
        ],
        out_specs=pl.BlockSpec((1, c, d), lambda b, i: (b, i, 0)),
        out_shape=jax.ShapeDtypeStruct(x.shape, F32),
        scratch_shapes=[pltpu.VMEM((HG_HEADS, HG_DK, HG_DK), F32), pltpu.VMEM((c, HG_WIDTH), BF16)],
        compiler_params=pltpu.CompilerParams(dimension_semantics=("arbitrary", "arbitrary"),
                                             vmem_limit_bytes=VMEM_LIMIT_BYTES),
        name="hgrn_layer",
    )(x, w_in, lb, gain, w_out, lng, lnb)


def _ffn_ple_kernel(x_ref, p_ref, wgu_ref, wd_ref, wpu_ref, wpg_ref, bpg_ref, lng_ref, lnb_ref, o_ref):
    sub = FFN_SUB_ROWS
    after = None
    for i in range(x_ref.shape[0] // sub):
        rows = slice(i * sub, (i + 1) * sub)
        x = x_ref[rows, :]
        lhs = x
        if after is not None:
            corner = jnp.where(after > jnp.inf, after, x[:SUBLANES, :LANES])
            lhs = jnp.concatenate([jnp.concatenate([corner, x[:SUBLANES, LANES:]], axis=1), x[SUBLANES:]], axis=0)
        gu = _dot(lhs.astype(BF16), wgu_ref[...].astype(BF16))
        gate, up = gu[:, :FFN_HIDDEN], gu[:, FFN_HIDDEN:]
        hid = (gate * _sigmoid(gate) * up).astype(BF16)
        down = _dot(hid, wd_ref[...].astype(BF16))
        after = down[sub - SUBLANES:, :LANES]
        x = _layer_norm(DEEPNORM_ALPHA * x + down, lng_ref[0:1, :], lnb_ref[0:1, :])
        ple_gate = _sigmoid(_dot(x.astype(BF16), wpg_ref[...].astype(BF16)) + bpg_ref[...])
        ple_up = _dot(p_ref[rows, :].astype(BF16), wpu_ref[...].astype(BF16))
        o_ref[rows, :] = _layer_norm(DEEPNORM_ALPHA * x + ple_gate * ple_up, lng_ref[1:2, :], lnb_ref[1:2, :])


def _ffn_ple(layer, x2d, p3d, wgu, wd, wpu, wpg, bpg, lng, lnb):
    t, d = x2d.shape
    rows = FFN_ROWS
    return pl.pallas_call(
        _ffn_ple_kernel,
        grid=(t // rows,),
        in_specs=[
            pl.BlockSpec((rows, d), lambda i: (i, 0)),
            pl.BlockSpec((None, rows, PLE_DIM), lambda i: (layer, i, 0)),
            _const_spec(wgu.shape, layer), _const_spec(wd.shape, layer), _const_spec(wpu.shape, layer),
            _const_spec(wpg.shape, layer), _const_spec(bpg.shape), _const_spec(lng.shape), _const_spec(lnb.shape),
        ],
        out_specs=pl.BlockSpec((rows, d), lambda i: (i, 0)),
        out_shape=jax.ShapeDtypeStruct(x2d.shape, F32),
        compiler_params=pltpu.CompilerParams(dimension_semantics=("arbitrary",),
                                             vmem_limit_bytes=VMEM_LIMIT_BYTES),
        name="ffn_ple",
    )(x2d, p3d, wgu, wd, wpu, wpg, bpg, lng, lnb)


def _swa_layer_kernel(slopes_ref, sinks_ref, x_ref, wkv_ref, bkv_ref, wq_ref, bq_ref, wo_ref, bo_ref,
                      lng_ref, lnb_ref, o_ref, kprev_ref, vprev_ref, att_ref, bias_ref):
    w = WINDOW
    rows = x_ref.shape[1]
    first = pl.program_id(1) == 0

    @pl.when((pl.program_id(0) == 0) & first)
    def _():
        qi = lax.broadcasted_iota(jnp.int32, (w, 2 * w), 0)
        si = lax.broadcasted_iota(jnp.int32, (w, 2 * w), 1)
        dist = qi - si + w
        band = (dist >= 0) & (dist < w)
        dist_f = dist.astype(F32)
        prev_keys = (si >= 1) & (si < w)
        for kh in range(ATT_KVH):
            for par in range(2):
                for r in range(2):
                    head = kh * ATT_G + 2 * r + par
                    bias = jnp.where(band, (slopes_ref[head] * dist_f) * (-LOG2E), -jnp.inf)
                    bias = jnp.where(si == 0, jnp.full((w, 2 * w), sinks_ref[head], F32) * LOG2E, bias)
                    bias_ref[0, kh, par, r * w:(r + 1) * w, :] = bias
                    bias_ref[1, kh, par, r * w:(r + 1) * w, :] = jnp.where(prev_keys, -jnp.inf, bias)

    @pl.when(first)
    def _():
        kprev_ref[...] = jnp.zeros_like(kprev_ref)
        vprev_ref[...] = jnp.zeros_like(vprev_ref)

    x = x_ref[0]
    xb = x.astype(BF16)
    kv = _dot(xb, wkv_ref[...].astype(BF16)) + bkv_ref[...]
    kd = kv[:, :ATT_KVH * LANES].astype(BF16)
    vd = kv[:, ATT_KVH * LANES:].astype(BF16)
    q = ((_dot(xb, wq_ref[...].astype(BF16)) + bq_ref[...]) * (ATT_HD ** -0.5 * LOG2E)).astype(BF16)

    lo_half = lax.broadcasted_iota(jnp.int32, (1, LANES), 1) < ATT_HD
    not_row0 = lax.broadcasted_iota(jnp.int32, (BF16_ROWS, 1), 0) > 0
    zero_bf = jnp.zeros((), BF16)
    first_idx = first.astype(jnp.int32)

    for j in range(rows // w):
        blk = slice(j * w, (j + 1) * w)
        if j == 0:
            k_prev, v_prev, tbl = kprev_ref[...], vprev_ref[...], first_idx
        else:
            prev = slice((j - 1) * w, j * w)
            k_prev, v_prev, tbl = kd[prev], vd[prev], 0
        keys = jnp.concatenate([jnp.where(not_row0, k_prev[:BF16_ROWS], zero_bf), k_prev[BF16_ROWS:], kd[blk]], axis=0)
        vals = jnp.concatenate([jnp.where(not_row0, v_prev[:BF16_ROWS], zero_bf), v_prev[BF16_ROWS:], vd[blk]], axis=0)
        for kh in range(ATT_KVH):
            grp = slice(kh * LANES, (kh + 1) * LANES)
            k_dup, v_dup = keys[:, grp], vals[:, grp]
            c0 = kh * ATT_G * ATT_HD
            q2 = jnp.concatenate([q[blk, c0:c0 + LANES], q[blk, c0 + LANES:c0 + 2 * LANES]], axis=0)
            out = None
            for par in range(2):
                keep = lo_half if par == 0 else jnp.logical_not(lo_half)
                s = _dot_nt(q2, jnp.where(keep, k_dup, zero_bf)) + bias_ref[tbl, kh, par]
                p = jnp.exp2(s - jnp.max(s, axis=-1, keepdims=True))
                inv = 1.0 / jnp.sum(p, axis=-1, keepdims=True)
                pv = _dot(p.astype(BF16), jnp.where(keep, v_dup, zero_bf)) * inv
                out = pv if out is None else out + pv
            att_ref[blk, c0:c0 + LANES] = out[:w].astype(BF16)
            att_ref[blk, c0 + LANES:c0 + 2 * LANES] = out[w:].astype(BF16)

    kprev_ref[...] = kd[rows - w:]
    vprev_ref[...] = vd[rows - w:]
    mix = _dot(att_ref[...], wo_ref[...].astype(BF16)) + bo_ref[...]
    o_ref[0] = _layer_norm(DEEPNORM_ALPHA * x + mix, lng_ref[...], lnb_ref[...])


def _swa_layer(x, slopes, sinks, wkv, bkv, wq, bq, wo, bo, lng, lnb):
    bsz, seq, d = x.shape
    rows = SWA_ROWS
    smem = pl.BlockSpec(memory_space=pltpu.SMEM)
    return pl.pallas_call(
        _swa_layer_kernel,
        grid=(bsz, seq // rows),
        in_specs=[
            smem, smem,
            pl.BlockSpec((1, rows, d), lambda b, i: (b, i, 0)),
            _const_spec(wkv.shape), _const_spec(bkv.shape), _const_spec(wq.shape, 0), _const_spec(bq.shape),
            _const_spec(wo.shape, 0), _const_spec(bo.shape), _const_spec(lng.shape), _const_spec(lnb.shape),
        ],
        out_specs=pl.BlockSpec((1, rows, d), lambda b, i: (b, i, 0)),
        out_shape=jax.ShapeDtypeStruct(x.shape, F32),
        scratch_shapes=[pltpu.VMEM((WINDOW, ATT_KVH * LANES), BF16), pltpu.VMEM((WINDOW, ATT_KVH * LANES), BF16),
                        pltpu.VMEM((rows, ATT_QH * ATT_HD), BF16),
                        pltpu.VMEM((2, ATT_KVH, 2, 2 * WINDOW, 2 * WINDOW), F32)],
        compiler_params=pltpu.CompilerParams(dimension_semantics=("arbitrary", "arbitrary"),
                                             vmem_limit_bytes=VMEM_LIMIT_BYTES),
        name="swa_layer",
    )(slopes, sinks, x, wkv, bkv, wq, bq, wo, bo, lng, lnb)


def _dup_heads(w):
    lead = w.shape[:-1]
    w4 = w.reshape(lead + (ATT_KVH, 1, ATT_HD))
    return jnp.broadcast_to(w4, lead + (ATT_KVH, 2, ATT_HD)).reshape(lead + (ATT_KVH * 2 * ATT_HD,))


def kernel(x, p, a_w_in, a_lower_bound, a_norm_gain, a_w_out, kv_w, kv_b, b_w_q, b_b_q, b_sinks, b_w_out, b_b_out,
           ffn_w_gate_up, ffn_w_down, ple_w_up, ple_w_gate, ple_b_gate, ln_gain, ln_bias):
    bsz, seq, d = x.shape
    row = lambda v: v.reshape(1, -1).astype(F32)
    lower_bounds = jnp.cumsum(jax.nn.softmax(a_lower_bound.astype(F32), axis=0), axis=0)
    p3d = p.reshape(DEPTH, bsz * seq, PLE_DIM)

    def ffn_ple(i, h):
        out = _ffn_ple(i, h.reshape(bsz * seq, d), p3d, ffn_w_gate_up, ffn_w_down, ple_w_up, ple_w_gate,
                       row(ple_b_gate[i]), ln_gain[i, 1:3], ln_bias[i, 1:3])
        return out.reshape(bsz, seq, d)

    h = _hgrn_layer(x, a_w_in, row(lower_bounds[0]), row(a_norm_gain[0]), a_w_out,
                    row(ln_gain[0, 0]), row(ln_bias[0, 0]))
    h = ffn_ple(0, h)

    kdim = ATT_KVH * ATT_HD
    wkv = jnp.concatenate([_dup_heads(kv_w[:, :kdim]), _dup_heads(kv_w[:, kdim:])], axis=-1)
    bkv = row(jnp.concatenate([_dup_heads(kv_b[:kdim]), _dup_heads(kv_b[kdim:])], axis=-1))
    slopes = jnp.exp2(-8.0 * jnp.arange(1, ATT_QH + 1, dtype=F32) / ATT_QH)
    h = _swa_layer(h, slopes, b_sinks[0].astype(F32), wkv, bkv, b_w_q, row(b_b_q[0]),
                   b_w_out, row(b_b_out[0]), row(ln_gain[1, 0]), row(ln_bias[1, 0]))
    h = ffn_ple(1, h)
    return h
```

```python
import functools

import jax
import jax.numpy as jnp
from jax import lax
from jax.experimental import pallas as pl
from jax.experimental.pallas import tpu as pltpu

F32 = jnp.float32
BF16 = jnp.bfloat16

D_MODEL = 1024
DEPTH = 2
HG_DK = 128
HG_HEADS = 8
HG_WIDTH = HG_HEADS * HG_DK
ATT_HD = 64
ATT_QH = 16
ATT_KVH = 4
ATT_G = ATT_QH // ATT_KVH
WINDOW = 128
FFN_HIDDEN = 2816
PLE_DIM = 256
DEEPNORM_ALPHA = (2.0 * DEPTH) ** 0.25
LN_EPS = 1e-5
RMS_EPS = 1e-6

LANES = 128
SUBLANES = 8
BF16_ROWS = 16
LOG2E = 1.4426950408889634
VMEM_LIMIT_BYTES = 56 * 1024 * 1024

HG_CHUNK = 128
HG_ROWS = 512
HG_LEVELS = (64, 32, 16, 8, 4, 2, 1)
HG_BLOCK = BF16_ROWS
HG_TWO_SIDED_MIN_LB = 2.0 ** -6
FFN_ROWS = 256
SWA_ROWS = 256


def _dot(a, b):
    return jnp.dot(a, b, preferred_element_type=F32)


def _dot_nt(a, b):
    return lax.dot_general(a, b, (((1,), (1,)), ((), ())), preferred_element_type=F32)


def _dot_tn(a, b):
    return lax.dot_general(a, b, (((0,), (0,)), ((), ())), preferred_element_type=F32)


def _sigmoid(x):
    return 1.0 / (1.0 + jnp.exp(-x))


def _layer_norm(x, g, b):
    mu = jnp.mean(x, axis=-1, keepdims=True)
    xc = x - mu
    var = jnp.mean(xc * xc, axis=-1, keepdims=True)
    return xc * lax.rsqrt(var + LN_EPS) * g + b


def _hgrn_level_exponent(n, b, lf, row, sign):
    c, width = b.shape
    if n >= 4:
        b3 = b.reshape(c // (2 * n), 2 * n, width)
        d = (b3 - b3[:, n - 1:n, :]).reshape(c, width)
        return d * sign
    if n == 2:
        p = row & 3
        nxt = pltpu.roll(lf, c - 1, axis=0)
        prv = pltpu.roll(lf, 1, axis=0)
        zero = jnp.zeros_like(lf)
        return (jnp.where(p == 0, nxt, zero) + jnp.where(p >= 2, lf, zero)
                + jnp.where(p == 3, prv, zero))
    return jnp.where((row & 1) == 1, lf, jnp.zeros_like(lf))


def _hgrn_level_operands(n, q_bf, k_bf, b, lf, row, signs):
    c = b.shape[0]
    if n < BF16_ROWS:
        e = jnp.exp2(_hgrn_level_exponent(n, b, lf, row, signs.get(n)).astype(BF16))
        return q_bf * e, k_bf * e
    q_parts, k_parts = [], []
    for g in range(c // (2 * n)):
        lo = slice(2 * n * g, 2 * n * g + n)
        up = slice(2 * n * g + n, 2 * n * (g + 1))
        ref_row = b[2 * n * g + n - 1:2 * n * g + n, :]
        q_parts.append(q_bf[up] * jnp.exp2((b[up] - ref_row).astype(BF16)))
        k_parts += [k_bf[lo] * jnp.exp2((ref_row - b[lo]).astype(BF16)), k_bf[up]]
    return jnp.concatenate(q_parts, axis=0), jnp.concatenate(k_parts, axis=0)


def _hgrn_merge_level(n, tiles, s_n, lvl_tiles):
    level = n.bit_length() - 1

    def merge(t, s_tile):
        old = jnp.zeros_like(s_tile) if tiles[t] is None else tiles[t]
        tiles[t] = jnp.where(lvl_tiles[t] == level, s_tile, old)

    if n >= BF16_ROWS:
        per = n // SUBLANES
        for i in range(s_n.shape[0] // SUBLANES):
            merge(2 * per * (i // per) + per + i % per, s_n[i * SUBLANES:(i + 1) * SUBLANES])
    else:
        for t in range(len(tiles)):
            if n == SUBLANES and t % 2 == 0:
                continue
            merge(t, s_n[t * SUBLANES:(t + 1) * SUBLANES])


def _hgrn_layer_kernel(two_sided, x_ref, w_in_ref, lb_ref, gain_ref, w_out_ref, lng_ref, lnb_ref,
                       o_ref, st_ref, y_ref):
    c = HG_CHUNK
    levels = tuple(n for n in HG_LEVELS if n >= HG_BLOCK) if two_sided else HG_LEVELS

    @pl.when(pl.program_id(1) == 0)
    def _():
        st_ref[...] = jnp.zeros_like(st_ref)

    w_in = w_in_ref[...].astype(BF16)
    w_out = w_out_ref[...].astype(BF16)
    lb = lb_ref[...]
    one_m_lb = 1.0 - lb
    gain = gain_ref[...]

    rr = lax.broadcasted_iota(jnp.int32, (c, c), 0)
    cc = lax.broadcasted_iota(jnp.int32, (c, c), 1)
    tri = (rr >= cc).astype(BF16)
    xr = rr ^ cc
    lvl = jnp.full((c, c), -1, jnp.int32)
    for j in range(len(HG_LEVELS)):
        lvl = lvl + (xr >= (1 << j)).astype(jnp.int32)
    lvl = jnp.where(rr > cc, lvl, -1)
    lvl_tiles = [lvl[t * SUBLANES:(t + 1) * SUBLANES] for t in range(c // SUBLANES)]
    in_block = ((rr // HG_BLOCK) == (cc // HG_BLOCK)) & (rr >= cc)
    block_tiles = [in_block[t * SUBLANES:(t + 1) * SUBLANES] for t in range(c // SUBLANES)]
    row = lax.broadcasted_iota(jnp.int32, (c, HG_WIDTH), 0)
    signs = {n: jnp.where(((row // n) & 1) == 1, 1.0, -1.0).astype(F32) for n in HG_LEVELS if 4 <= n < BF16_ROWS}
    heads = [slice(h * HG_DK, (h + 1) * HG_DK) for h in range(HG_HEADS)]

    def project(ci, after=None):
        x = x_ref[0, ci * c:(ci + 1) * c, :]
        lhs = x
        if after is not None:
            dep = after[c - SUBLANES:, :LANES]
            corner = jnp.where(dep > jnp.inf, dep, x[:SUBLANES, :LANES])
            lhs = jnp.concatenate([jnp.concatenate([corner, x[:SUBLANES, LANES:]], axis=1), x[SUBLANES:]], axis=0)
        return x, _dot(lhs.astype(BF16), w_in)

    def gates(proj):
        sig_f = _sigmoid(proj[:, HG_WIDTH:2 * HG_WIDTH])
        log_f = jnp.log2(lb + one_m_lb * sig_f)
        k_all = one_m_lb * (1.0 - sig_f)
        q_raw = proj[:, :HG_WIDTH]
        q_all = q_raw * _sigmoid(q_raw) * (HG_DK ** -0.5)
        v_all = proj[:, 2 * HG_WIDTH:3 * HG_WIDTH]
        g_raw = proj[:, 3 * HG_WIDTH:]
        g_all = g_raw * _sigmoid(g_raw)
        hi = log_f.astype(BF16)
        r1 = log_f - hi.astype(F32)
        mid = r1.astype(BF16)
        lo = (r1 - mid.astype(F32)).astype(BF16)
        b_all = _dot(tri, hi) + _dot(tri, mid) + _dot(tri, lo)
        return q_all, k_all, log_f, b_all, v_all, g_all

    def level_operands(vals):
        q_all, k_all, log_f, b_all, v_all, _ = vals
        q_bf, k_bf, v_bf = q_all.astype(BF16), k_all.astype(BF16), v_all.astype(BF16)
        operands = {n: _hgrn_level_operands(n, q_bf, k_bf, b_all, log_f, row, signs) for n in levels}
        b_last = b_all[c - 1:c, :]
        q_dec = q_bf * jnp.exp2(b_all.astype(BF16))
        k_dec = k_bf * jnp.exp2((b_last - b_all).astype(BF16))
        if two_sided:
            b3 = b_all.reshape(c // HG_BLOCK, HG_BLOCK, HG_WIDTH)
            ref = jnp.concatenate([jnp.zeros((1, 1, HG_WIDTH), F32), b3[:-1, HG_BLOCK - 1:, :]], axis=0)
            rel = (b3 - ref).reshape(c, HG_WIDTH)
            within = (q_bf * jnp.exp2(rel).astype(BF16), k_bf * jnp.exp2(-rel).astype(BF16))
        else:
            within = q_all * k_all
        return operands, q_dec, k_dec, v_bf, jnp.exp2(b_last), within

    def scores_and_readout(vals, ops):
        v_all = vals[4]
        operands, q_dec, k_dec, v_bf, st_decay, within = ops
        tiles = [[None] * (c // SUBLANES) for _ in heads]
        for n in levels:
            q_n, k_n = operands[n]
            for h, sl in enumerate(heads):
                _hgrn_merge_level(n, tiles[h], _dot_nt(q_n[:, sl], k_n[:, sl]), lvl_tiles)
        if two_sided:
            for h, sl in enumerate(heads):
                s_blk = _dot_nt(within[0][:, sl], within[1][:, sl])
                for t, mask in enumerate(block_tiles):
                    s_tile = s_blk[t * SUBLANES:(t + 1) * SUBLANES]
                    old = jnp.zeros_like(s_tile) if tiles[h][t] is None else tiles[h][t]
                    tiles[h][t] = jnp.where(mask, s_tile, old)
        outs = []
        for h, sl in enumerate(heads):
            st = st_ref[h]
            o_h = _dot(jnp.concatenate(tiles[h], axis=0).astype(BF16), v_bf[:, sl])
            o_h = o_h + _dot_nt(q_dec[:, sl], st.astype(BF16))
            st_ref[h] = st * st_decay[:, sl] + _dot_tn(v_bf[:, sl], k_dec[:, sl])
            if not two_sided:
                o_h = o_h + jnp.sum(within[:, sl], axis=-1, keepdims=True) * v_all[:, sl]
            outs.append(o_h)
        return outs

    def finish(ci, x, g_all, outs):
        rows_c = slice(ci * c, (ci + 1) * c)
        for h, sl in enumerate(heads):
            ms = jnp.mean(outs[h] * outs[h], axis=-1, keepdims=True)
            y_ref[rows_c, sl] = (outs[h] * lax.rsqrt(ms + RMS_EPS) * gain * g_all[:, sl]).astype(BF16)
        mix = _dot(y_ref[rows_c, :], w_out)
        o_ref[0, rows_c, :] = _layer_norm(DEEPNORM_ALPHA * x + mix, lng_ref[...], lnb_ref[...])

    n_chunks = x_ref.shape[1] // c
    x_cur, proj = project(0)
    vals = gates(proj)
    for ci in range(n_chunks):
        ops = level_operands(vals)
        if ci + 1 < n_chunks:
            x_next, proj = project(ci + 1, after=vals[3])
        outs = scores_and_readout(vals, ops)
        g_cur = vals[5]
        if ci + 1 < n_chunks:
            vals = gates(proj)
        finish(ci, x_cur, g_cur, outs)
        if ci + 1 < n_chunks:
            x_cur = x_next


def _const_spec(shape, layer=None):
    if layer is None:
        zeros = (0,) * len(shape)
        return pl.BlockSpec(shape, lambda *_: zeros, pipeline_mode=pl.Buffered(1))
    zeros = (0,) * (len(shape) - 1)
    return pl.BlockSpec((None,) + tuple(shape[1:]), lambda *_: (layer,) + zeros, pipeline_mode=pl.Buffered(1))


def _hgrn_layer(two_sided, x, w_in, lb, gain, w_out, lng, lnb):
    bsz, seq, d = x.shape
    c = HG_ROWS
    return pl.pallas_call(
        functools.partial(_hgrn_layer_kernel, two_sided),
        grid=(bsz, seq // c),
        in_specs=[
            pl.BlockSpec((1, c, d), lambda b, i: (b, i, 0)),
            _const_spec(w_in.shape, 0), _const_spec(lb.shape), _const_spec(gain.shape),
            _const_spec(w_out.shape, 0), _const_spec(lng.shape), _const_spec(lnb.shape),
        ],
        out_specs=pl.BlockSpec((1, c, d), lambda b, i: (b, i, 0)),
        out_shape=jax.ShapeDtypeStruct(x.shape, F32),
        scratch_shapes=[pltpu.VMEM((HG_HEADS, HG_DK, HG_DK), F32), pltpu.VMEM((c, HG_WIDTH), BF16)],
        compiler_params=pltpu.CompilerParams(dimension_semantics=("arbitrary", "arbitrary"),
                                             vmem_limit_bytes=VMEM_LIMIT_BYTES),
        name="hgrn_layer",
    )(x, w_in, lb, gain, w_out, lng, lnb)


def _ffn_ple_kernel(x_ref, p_ref, wgu_ref, wd_ref, wpu_ref, wpg_ref, bpg_ref, lng_ref, lnb_ref, o_ref):
    x = x_ref[...]
    gu = _dot(x.astype(BF16), wgu_ref[...].astype(BF16))
    gate, up = gu[:, :FFN_HIDDEN], gu[:, FFN_HIDDEN:]
    hid = (gate * _sigmoid(gate) * up).astype(BF16)
    x = _layer_norm(DEEPNORM_ALPHA * x + _dot(hid, wd_ref[...].astype(BF16)), lng_ref[0:1, :], lnb_ref[0:1, :])
    ple_gate = _sigmoid(_dot(x.astype(BF16), wpg_ref[...].astype(BF16)) + bpg_ref[...])
    ple_up = _dot(p_ref[...].astype(BF16), wpu_ref[...].astype(BF16))
    o_ref[...] = _layer_norm(DEEPNORM_ALPHA * x + ple_gate * ple_up, lng_ref[1:2, :], lnb_ref[1:2, :])


def _ffn_ple(layer, x2d, p3d, wgu, wd, wpu, wpg, bpg, lng, lnb):
    t, d = x2d.shape
    rows = FFN_ROWS
    return pl.pallas_call(
        _ffn_ple_kernel,
        grid=(t // rows,),
        in_specs=[
            pl.BlockSpec((rows, d), lambda i: (i, 0)),
            pl.BlockSpec((None, rows, PLE_DIM), lambda i: (layer, i, 0)),
            _const_spec(wgu.shape, layer), _const_spec(wd.shape, layer), _const_spec(wpu.shape, layer),
            _const_spec(wpg.shape, layer), _const_spec(bpg.shape), _const_spec(lng.shape), _const_spec(lnb.shape),
        ],
        out_specs=pl.BlockSpec((rows, d), lambda i: (i, 0)),
        out_shape=jax.ShapeDtypeStruct(x2d.shape, F32),
        compiler_params=pltpu.CompilerParams(dimension_semantics=("arbitrary",),
                                             vmem_limit_bytes=VMEM_LIMIT_BYTES),
        name="ffn_ple",
    )(x2d, p3d, wgu, wd, wpu, wpg, bpg, lng, lnb)


def _swa_layer_kernel(slopes_ref, sinks_ref, x_ref, wkv_ref, bkv_ref, wq_ref, bq_ref, wo_ref, bo_ref,
                      lng_ref, lnb_ref, o_ref, kprev_ref, vprev_ref, att_ref, bias_ref):
    w = WINDOW
    rows = x_ref.shape[1]
    first = pl.program_id(1) == 0

    @pl.when((pl.program_id(0) == 0) & first)
    def _():
        qi = lax.broadcasted_iota(jnp.int32, (w, 2 * w), 0)
        si = lax.broadcasted_iota(jnp.int32, (w, 2 * w), 1)
        dist = qi - si + w
        band = (dist >= 0) & (dist < w)
        dist_f = dist.astype(F32)
        prev_keys = (si >= 1) & (si < w)
        for kh in range(ATT_KVH):
            for par in range(2):
                for r in range(2):
                    head = kh * ATT_G + 2 * r + par
                    bias = jnp.where(band, (slopes_ref[head] * dist_f) * (-LOG2E), -jnp.inf)
                    bias = jnp.where(si == 0, jnp.full((w, 2 * w), sinks_ref[head], F32) * LOG2E, bias)
                    bias_ref[0, kh, par, r * w:(r + 1) * w, :] = bias
                    bias_ref[1, kh, par, r * w:(r + 1) * w, :] = jnp.where(prev_keys, -jnp.inf, bias)

    @pl.when(first)
    def _():
        kprev_ref[...] = jnp.zeros_like(kprev_ref)
        vprev_ref[...] = jnp.zeros_like(vprev_ref)

    x = x_ref[0]
    xb = x.astype(BF16)
    kv = _dot(xb, wkv_ref[...].astype(BF16)) + bkv_ref[...]
    kd = kv[:, :ATT_KVH * LANES].astype(BF16)
    vd = kv[:, ATT_KVH * LANES:].astype(BF16)
    q = ((_dot(xb, wq_ref[...].astype(BF16)) + bq_ref[...]) * (ATT_HD ** -0.5 * LOG2E)).astype(BF16)

    lo_half = lax.broadcasted_iota(jnp.int32, (1, LANES), 1) < ATT_HD
    not_row0 = lax.broadcasted_iota(jnp.int32, (BF16_ROWS, 1), 0) > 0
    zero_bf = jnp.zeros((), BF16)
    first_idx = first.astype(jnp.int32)

    for j in range(rows // w):
        blk = slice(j * w, (j + 1) * w)
        if j == 0:
            k_prev, v_prev, tbl = kprev_ref[...], vprev_ref[...], first_idx
        else:
            prev = slice((j - 1) * w, j * w)
            k_prev, v_prev, tbl = kd[prev], vd[prev], 0
        keys = jnp.concatenate([jnp.where(not_row0, k_prev[:BF16_ROWS], zero_bf), k_prev[BF16_ROWS:], kd[blk]], axis=0)
        vals = jnp.concatenate([jnp.where(not_row0, v_prev[:BF16_ROWS], zero_bf), v_prev[BF16_ROWS:], vd[blk]], axis=0)
        for kh in range(ATT_KVH):
            grp = slice(kh * LANES, (kh + 1) * LANES)
            k_dup, v_dup = keys[:, grp], vals[:, grp]
            c0 = kh * ATT_G * ATT_HD
            q2 = jnp.concatenate([q[blk, c0:c0 + LANES], q[blk, c0 + LANES:c0 + 2 * LANES]], axis=0)
            out = None
            for par in range(2):
                keep = lo_half if par == 0 else jnp.logical_not(lo_half)
                s = _dot_nt(q2, jnp.where(keep, k_dup, zero_bf)) + bias_ref[tbl, kh, par]
                p = jnp.exp2(s - jnp.max(s, axis=-1, keepdims=True))
                inv = 1.0 / jnp.sum(p, axis=-1, keepdims=True)
                pv = _dot(p.astype(BF16), jnp.where(keep, v_dup, zero_bf)) * inv
                out = pv if out is None else out + pv
            att_ref[blk, c0:c0 + LANES] = out[:w].astype(BF16)
            att_ref[blk, c0 + LANES:c0 + 2 * LANES] = out[w:].astype(BF16)

    kprev_ref[...] = kd[rows - w:]
    vprev_ref[...] = vd[rows - w:]
    mix = _dot(att_ref[...], wo_ref[...].astype(BF16)) + bo_ref[...]
    o_ref[0] = _layer_norm(DEEPNORM_ALPHA * x + mix, lng_ref[...], lnb_ref[...])


def _swa_layer(x, slopes, sinks, wkv, bkv, wq, bq, wo, bo, lng, lnb):
    bsz, seq, d = x.shape
    rows = SWA_ROWS
    smem = pl.BlockSpec(memory_space=pltpu.SMEM)
    return pl.pallas_call(
        _swa_layer_kernel,
        grid=(bsz, seq // rows),
        in_specs=[
            smem, smem,
            pl.BlockSpec((1, rows, d), lambda b, i: (b, i, 0)),
            _const_spec(wkv.shape), _const_spec(bkv.shape), _const_spec(wq.shape, 0), _const_spec(bq.shape),
            _const_spec(wo.shape, 0), _const_spec(bo.shape), _const_spec(lng.shape), _const_spec(lnb.shape),
        ],
        out_specs=pl.BlockSpec((1, rows, d), lambda b, i: (b, i, 0)),
        out_shape=jax.ShapeDtypeStruct(x.shape, F32),
        scratch_shapes=[pltpu.VMEM((WINDOW, ATT_KVH * LANES), BF16), pltpu.VMEM((WINDOW, ATT_KVH * LANES), BF16),
                        pltpu.VMEM((rows, ATT_QH * ATT_HD), BF16),
                        pltpu.VMEM((2, ATT_KVH, 2, 2 * WINDOW, 2 * WINDOW), F32)],
        compiler_params=pltpu.CompilerParams(dimension_semantics=("arbitrary", "arbitrary"),
                                             vmem_limit_bytes=VMEM_LIMIT_BYTES),
        name="swa_layer",
    )(slopes, sinks, x, wkv, bkv, wq, bq, wo, bo, lng, lnb)


def _dup_heads(w):
    lead = w.shape[:-1]
    w4 = w.reshape(lead + (ATT_KVH, 1, ATT_HD))
    return jnp.broadcast_to(w4, lead + (ATT_KVH, 2, ATT_HD)).reshape(lead + (ATT_KVH * 2 * ATT_HD,))


def kernel(x, p, a_w_in, a_lower_bound, a_norm_gain, a_w_out, kv_w, kv_b, b_w_q, b_b_q, b_sinks, b_w_out, b_b_out,
           ffn_w_gate_up, ffn_w_down, ple_w_up, ple_w_gate, ple_b_gate, ln_gain, ln_bias):
    bsz, seq, d = x.shape
    row = lambda v: v.reshape(1, -1).astype(F32)
    lower_bounds = jnp.cumsum(jax.nn.softmax(a_lower_bound.astype(F32), axis=0), axis=0)
    p3d = p.reshape(DEPTH, bsz * seq, PLE_DIM)

    def ffn_ple(i, h):
        out = _ffn_ple(i, h.reshape(bsz * seq, d), p3d, ffn_w_gate_up, ffn_w_down, ple_w_up, ple_w_gate,
                       row(ple_b_gate[i]), ln_gain[i, 1:3], ln_bias[i, 1:3])
        return out.reshape(bsz, seq, d)

    hgrn_args = (x, a_w_in, row(lower_bounds[0]), row(a_norm_gain[0]), a_w_out, row(ln_gain[0, 0]), row(ln_bias[0, 0]))
    h = lax.cond(jnp.min(lower_bounds[0]) >= HG_TWO_SIDED_MIN_LB,
                 functools.partial(_hgrn_layer, True), functools.partial(_hgrn_layer, False), *hgrn_args)
    h = ffn_ple(0, h)

    kdim = ATT_KVH * ATT_HD
    wkv = jnp.concatenate([_dup_heads(kv_w[:, :kdim]), _dup_heads(kv_w[:, kdim:])], axis=-1)
    bkv = row(jnp.concatenate([_dup_heads(kv_b[:kdim]), _dup_heads(kv_b[kdim:])], axis=-1))
    slopes = jnp.exp2(-8.0 * jnp.arange(1, ATT_QH + 1, dtype=F32) / ATT_QH)
    h = _swa_layer(h, slopes, b_sinks[0].astype(F32), wkv, bkv, b_w_q, row(b_b_q[0]),
                   b_w_out, row(b_b_out[0]), row(ln_gain[1, 0]), row(ln_bias[1, 0]))
    h = ffn_ple(1, h)
    return h
```

```python
import functools

import jax
import jax.numpy as jnp
from jax import lax
from jax.experimental import pallas as pl
from jax.experimental.pallas import tpu as pltpu

F32 = jnp.float32
BF16 = jnp.bfloat16

D_MODEL = 1024
DEPTH = 2
HG_DK = 128
HG_HEADS = 8
HG_WIDTH = HG_HEADS * HG_DK
ATT_HD = 64
ATT_QH = 16
ATT_KVH = 4
ATT_G = ATT_QH // ATT_KVH
WINDOW = 128
FFN_HIDDEN = 2816
PLE_DIM = 256
DEEPNORM_ALPHA = (2.0 * DEPTH) ** 0.25
LN_EPS = 1e-5
RMS_EPS = 1e-6

LANES = 128
SUBLANES = 8
BF16_ROWS = 16
LOG2E = 1.4426950408889634
VMEM_LIMIT_BYTES = 56 * 1024 * 1024

HG_CHUNK = 128
HG_ROWS = 512
HG_LEVELS = (64, 32, 16, 8, 4, 2, 1)
HG_BLOCK = BF16_ROWS
HG_TWO_SIDED_MIN_LB = 2.0 ** -6
FFN_ROWS = 256
SWA_ROWS = 256


def _dot(a, b):
    return jnp.dot(a, b, preferred_element_type=F32)


def _dot_nt(a, b):
    return lax.dot_general(a, b, (((1,), (1,)), ((), ())), preferred_element_type=F32)


def _dot_tn(a, b):
    return lax.dot_general(a, b, (((0,), (0,)), ((), ())), preferred_element_type=F32)


def _sigmoid(x):
    return 1.0 / (1.0 + jnp.exp(-x))


def _schedule_after(x, dep):
    corner = jnp.where(dep > jnp.inf, dep, x[:SUBLANES, :LANES])
    return jnp.concatenate([jnp.concatenate([corner, x[:SUBLANES, LANES:]], axis=1), x[SUBLANES:]], axis=0)


def _layer_norm(x, g, b):
    mu = jnp.mean(x, axis=-1, keepdims=True)
    xc = x - mu
    var = jnp.mean(xc * xc, axis=-1, keepdims=True)
    return xc * lax.rsqrt(var + LN_EPS) * g + b


def _hgrn_level_exponent(n, b, lf, row, sign):
    c, width = b.shape
    if n >= 4:
        b3 = b.reshape(c // (2 * n), 2 * n, width)
        d = (b3 - b3[:, n - 1:n, :]).reshape(c, width)
        return d * sign
    if n == 2:
        p = row & 3
        nxt = pltpu.roll(lf, c - 1, axis=0)
        prv = pltpu.roll(lf, 1, axis=0)
        zero = jnp.zeros_like(lf)
        return (jnp.where(p == 0, nxt, zero) + jnp.where(p >= 2, lf, zero)
                + jnp.where(p == 3, prv, zero))
    return jnp.where((row & 1) == 1, lf, jnp.zeros_like(lf))


def _hgrn_level_operands(n, q_bf, k_bf, b, lf, row, signs):
    c = b.shape[0]
    if n < BF16_ROWS:
        e = jnp.exp2(_hgrn_level_exponent(n, b, lf, row, signs.get(n)).astype(BF16))
        return q_bf * e, k_bf * e
    q_parts, k_parts = [], []
    for g in range(c // (2 * n)):
        lo = slice(2 * n * g, 2 * n * g + n)
        up = slice(2 * n * g + n, 2 * n * (g + 1))
        ref_row = b[2 * n * g + n - 1:2 * n * g + n, :]
        q_parts.append(q_bf[up] * jnp.exp2((b[up] - ref_row).astype(BF16)))
        k_parts += [k_bf[lo] * jnp.exp2((ref_row - b[lo]).astype(BF16)), k_bf[up]]
    return jnp.concatenate(q_parts, axis=0), jnp.concatenate(k_parts, axis=0)


def _hgrn_merge_level(n, tiles, s_n, lvl_tiles):
    level = n.bit_length() - 1

    def merge(t, s_tile):
        old = jnp.zeros_like(s_tile) if tiles[t] is None else tiles[t]
        tiles[t] = jnp.where(lvl_tiles[t] == level, s_tile, old)

    if n >= BF16_ROWS:
        per = n // SUBLANES
        for i in range(s_n.shape[0] // SUBLANES):
            merge(2 * per * (i // per) + per + i % per, s_n[i * SUBLANES:(i + 1) * SUBLANES])
    else:
        for t in range(len(tiles)):
            if n == SUBLANES and t % 2 == 0:
                continue
            merge(t, s_n[t * SUBLANES:(t + 1) * SUBLANES])


def _hgrn_layer_kernel(two_sided, x_ref, w_in_ref, lb_ref, gain_ref, w_out_ref, lng_ref, lnb_ref,
                       o_ref, st_ref, y_ref):
    c = HG_CHUNK
    levels = tuple(n for n in HG_LEVELS if n >= HG_BLOCK) if two_sided else HG_LEVELS

    @pl.when(pl.program_id(1) == 0)
    def _():
        st_ref[...] = jnp.zeros_like(st_ref)

    w_in = w_in_ref[...].astype(BF16)
    w_out = w_out_ref[...].astype(BF16)
    lb = lb_ref[...]
    one_m_lb = 1.0 - lb
    gain = gain_ref[...]

    rr = lax.broadcasted_iota(jnp.int32, (c, c), 0)
    cc = lax.broadcasted_iota(jnp.int32, (c, c), 1)
    tri = (rr >= cc).astype(BF16)
    xr = rr ^ cc
    lvl = jnp.full((c, c), -1, jnp.int32)
    for j in range(len(HG_LEVELS)):
        lvl = lvl + (xr >= (1 << j)).astype(jnp.int32)
    lvl = jnp.where(rr > cc, lvl, -1)
    lvl_tiles = [lvl[t * SUBLANES:(t + 1) * SUBLANES] for t in range(c // SUBLANES)]
    in_block = ((rr // HG_BLOCK) == (cc // HG_BLOCK)) & (rr >= cc)
    block_tiles = [in_block[t * SUBLANES:(t + 1) * SUBLANES] for t in range(c // SUBLANES)]
    row = lax.broadcasted_iota(jnp.int32, (c, HG_WIDTH), 0)
    signs = {n: jnp.where(((row // n) & 1) == 1, 1.0, -1.0).astype(F32) for n in HG_LEVELS if 4 <= n < BF16_ROWS}
    heads = [slice(h * HG_DK, (h + 1) * HG_DK) for h in range(HG_HEADS)]

    def project(ci, after=None):
        x = x_ref[0, ci * c:(ci + 1) * c, :]
        lhs = x if after is None else _schedule_after(x, after[c - SUBLANES:, :LANES])
        return x, _dot(lhs.astype(BF16), w_in)

    def gates(proj):
        gated = one_m_lb * _sigmoid(proj[:, HG_WIDTH:2 * HG_WIDTH])
        log_f = jnp.log2(lb + gated)
        k_all = one_m_lb - gated
        q_raw = proj[:, :HG_WIDTH]
        q_all = q_raw * _sigmoid(q_raw) * (HG_DK ** -0.5)
        v_all = proj[:, 2 * HG_WIDTH:3 * HG_WIDTH]
        g_raw = proj[:, 3 * HG_WIDTH:]
        g_all = g_raw * _sigmoid(g_raw)
        hi = log_f.astype(BF16)
        lo = (log_f - hi.astype(F32)).astype(BF16)
        b_all = _dot(tri, hi) + _dot(tri, lo)
        return q_all, k_all, log_f, b_all, v_all, g_all

    def level_operands(vals):
        q_all, k_all, log_f, b_all, v_all, _ = vals
        q_bf, k_bf, v_bf = q_all.astype(BF16), k_all.astype(BF16), v_all.astype(BF16)
        operands = {n: _hgrn_level_operands(n, q_bf, k_bf, b_all, log_f, row, signs) for n in levels}
        b_last = b_all[c - 1:c, :]
        q_dec = q_bf * jnp.exp2(b_all.astype(BF16))
        k_dec = k_bf * jnp.exp2((b_last - b_all).astype(BF16))
        if two_sided:
            b3 = b_all.reshape(c // HG_BLOCK, HG_BLOCK, HG_WIDTH)
            ref = jnp.concatenate([jnp.zeros((1, 1, HG_WIDTH), F32), b3[:-1, HG_BLOCK - 1:, :]], axis=0)
            rel = (b3 - ref).reshape(c, HG_WIDTH)
            within = (q_bf * jnp.exp2(rel).astype(BF16), k_bf * jnp.exp2(-rel).astype(BF16))
        else:
            within = q_all * k_all
        return operands, q_dec, k_dec, v_bf, jnp.exp2(b_last), within

    def scores_and_readout(vals, ops):
        v_all = vals[4]
        operands, q_dec, k_dec, v_bf, st_decay, within = ops
        tiles = [[None] * (c // SUBLANES) for _ in heads]
        for n in levels:
            q_n, k_n = operands[n]
            for h, sl in enumerate(heads):
                _hgrn_merge_level(n, tiles[h], _dot_nt(q_n[:, sl], k_n[:, sl]), lvl_tiles)
        if two_sided:
            for h, sl in enumerate(heads):
                s_blk = _dot_nt(within[0][:, sl], within[1][:, sl])
                for t, mask in enumerate(block_tiles):
                    s_tile = s_blk[t * SUBLANES:(t + 1) * SUBLANES]
                    old = jnp.zeros_like(s_tile) if tiles[h][t] is None else tiles[h][t]
                    tiles[h][t] = jnp.where(mask, s_tile, old)
        outs = []
        for h, sl in enumerate(heads):
            st = st_ref[h]
            o_h = _dot(jnp.concatenate(tiles[h], axis=0).astype(BF16), v_bf[:, sl])
            o_h = o_h + _dot_nt(q_dec[:, sl], st.astype(BF16))
            st_ref[h] = st * st_decay[:, sl] + _dot_tn(v_bf[:, sl], k_dec[:, sl])
            if not two_sided:
                o_h = o_h + jnp.sum(within[:, sl], axis=-1, keepdims=True) * v_all[:, sl]
            outs.append(o_h)
        return outs

    def finish(ci, x, g_all, outs):
        rows_c = slice(ci * c, (ci + 1) * c)
        for h, sl in enumerate(heads):
            ms = jnp.mean(outs[h] * outs[h], axis=-1, keepdims=True)
            y_ref[rows_c, sl] = (outs[h] * lax.rsqrt(ms + RMS_EPS) * gain * g_all[:, sl]).astype(BF16)
        mix = _dot(y_ref[rows_c, :], w_out)
        o_ref[0, rows_c, :] = _layer_norm(DEEPNORM_ALPHA * x + mix, lng_ref[...], lnb_ref[...])

    n_chunks = x_ref.shape[1] // c
    x_cur, proj = project(0)
    vals = gates(proj)
    for ci in range(n_chunks):
        ops = level_operands(vals)
        if ci + 1 < n_chunks:
            x_next, proj = project(ci + 1, after=vals[3])
        outs = scores_and_readout(vals, ops)
        g_cur = vals[5]
        if ci + 1 < n_chunks:
            vals = gates(proj)
        finish(ci, x_cur, g_cur, outs)
        if ci + 1 < n_chunks:
            x_cur = x_next


def _const_spec(shape, layer=None):
    if layer is None:
        zeros = (0,) * len(shape)
        return pl.BlockSpec(shape, lambda *_: zeros, pipeline_mode=pl.Buffered(1))
    zeros = (0,) * (len(shape) - 1)
    return pl.BlockSpec((None,) + tuple(shape[1:]), lambda *_: (layer,) + zeros, pipeline_mode=pl.Buffered(1))


def _hgrn_layer(two_sided, x, w_in, lb, gain, w_out, lng, lnb):
    bsz, seq, d = x.shape
    c = HG_ROWS
    return pl.pallas_call(
        functools.partial(_hgrn_layer_kernel, two_sided),
        grid=(bsz, seq // c),
        in_specs=[
            pl.BlockSpec((1, c, d), lambda b, i: (b, i, 0)),
            _const_spec(w_in.shape, 0), _const_spec(lb.shape), _const_spec(gain.shape),
            _const_spec(w_out.shape, 0), _const_spec(lng.shape), _const_spec(lnb.shape),
        ],
        out_specs=pl.BlockSpec((1, c, d), lambda b, i: (b, i, 0)),
        out_shape=jax.ShapeDtypeStruct(x.shape, F32),
        scratch_shapes=[pltpu.VMEM((HG_HEADS, HG_DK, HG_DK), F32), pltpu.VMEM((c, HG_WIDTH), BF16)],
        compiler_params=pltpu.CompilerParams(dimension_semantics=("arbitrary", "arbitrary"),
                                             vmem_limit_bytes=VMEM_LIMIT_BYTES),
        name="hgrn_layer",
    )(x, w_in, lb, gain, w_out, lng, lnb)


def _ffn_ple_kernel(x_ref, p_ref, wgu_ref, wd_ref, wpu_ref, wpg_ref, bpg_ref, lng_ref, lnb_ref, o_ref):
    x = x_ref[...]
    gu = _dot(x.astype(BF16), wgu_ref[...].astype(BF16))
    gate, up = gu[:, :FFN_HIDDEN], gu[:, FFN_HIDDEN:]
    hid = (gate * _sigmoid(gate) * up).astype(BF16)
    x = _layer_norm(DEEPNORM_ALPHA * x + _dot(hid, wd_ref[...].astype(BF16)), lng_ref[0:1, :], lnb_ref[0:1, :])
    ple_gate = _sigmoid(_dot(x.astype(BF16), wpg_ref[...].astype(BF16)) + bpg_ref[...])
    ple_up = _dot(p_ref[...].astype(BF16), wpu_ref[...].astype(BF16))
    o_ref[...] = _layer_norm(DEEPNORM_ALPHA * x + ple_gate * ple_up, lng_ref[1:2, :], lnb_ref[1:2, :])


def _ffn_ple(layer, x2d, p3d, wgu, wd, wpu, wpg, bpg, lng, lnb):
    t, d = x2d.shape
    rows = FFN_ROWS
    return pl.pallas_call(
        _ffn_ple_kernel,
        grid=(t // rows,),
        in_specs=[
            pl.BlockSpec((rows, d), lambda i: (i, 0)),
            pl.BlockSpec((None, rows, PLE_DIM), lambda i: (layer, i, 0)),
            _const_spec(wgu.shape, layer), _const_spec(wd.shape, layer), _const_spec(wpu.shape, layer),
            _const_spec(wpg.shape, layer), _const_spec(bpg.shape), _const_spec(lng.shape), _const_spec(lnb.shape),
        ],
        out_specs=pl.BlockSpec((rows, d), lambda i: (i, 0)),
        out_shape=jax.ShapeDtypeStruct(x2d.shape, F32),
        compiler_params=pltpu.CompilerParams(dimension_semantics=("arbitrary",),
                                             vmem_limit_bytes=VMEM_LIMIT_BYTES),
        name="ffn_ple",
    )(x2d, p3d, wgu, wd, wpu, wpg, bpg, lng, lnb)


def _swa_layer_kernel(slopes_ref, sinks_ref, x_ref, wkv_ref, bkv_ref, wq_ref, bq_ref, wo_ref, bo_ref,
                      lng_ref, lnb_ref, o_ref, kprev_ref, vprev_ref, att_ref, bias_ref):
    w = WINDOW
    rows = x_ref.shape[1]
    first = pl.program_id(1) == 0

    @pl.when((pl.program_id(0) == 0) & first)
    def _():
        qi = lax.broadcasted_iota(jnp.int32, (w, 2 * w), 0)
        si = lax.broadcasted_iota(jnp.int32, (w, 2 * w), 1)
        dist = qi - si + w
        band = (dist >= 0) & (dist < w)
        dist_f = dist.astype(F32)
        prev_keys = (si >= 1) & (si < w)
        for kh in range(ATT_KVH):
            for par in range(2):
                for r in range(2):
                    head = kh * ATT_G + 2 * r + par
                    bias = jnp.where(band, (slopes_ref[head] * dist_f) * (-LOG2E), -jnp.inf)
                    bias = jnp.where(si == 0, jnp.full((w, 2 * w), sinks_ref[head], F32) * LOG2E, bias)
                    bias_ref[0, kh, par, r * w:(r + 1) * w, :] = bias
                    bias_ref[1, kh, par, r * w:(r + 1) * w, :] = jnp.where(prev_keys, -jnp.inf, bias)

    @pl.when(first)
    def _():
        kprev_ref[...] = jnp.zeros_like(kprev_ref)
        vprev_ref[...] = jnp.zeros_like(vprev_ref)

    x = x_ref[0]
    xb = x.astype(BF16)
    kv = _dot(xb, wkv_ref[...].astype(BF16)) + bkv_ref[...]
    kd = kv[:, :ATT_KVH * LANES].astype(BF16)
    vd = kv[:, ATT_KVH * LANES:].astype(BF16)
    q = ((_dot(xb, wq_ref[...].astype(BF16)) + bq_ref[...]) * (ATT_HD ** -0.5 * LOG2E)).astype(BF16)

    lo_half = lax.broadcasted_iota(jnp.int32, (1, LANES), 1) < ATT_HD
    not_row0 = lax.broadcasted_iota(jnp.int32, (BF16_ROWS, 1), 0) > 0
    zero_bf = jnp.zeros((), BF16)
    first_idx = first.astype(jnp.int32)

    for j in range(rows // w):
        blk = slice(j * w, (j + 1) * w)
        if j == 0:
            k_prev, v_prev, tbl = kprev_ref[...], vprev_ref[...], first_idx
        else:
            prev = slice((j - 1) * w, j * w)
            k_prev, v_prev, tbl = kd[prev], vd[prev], 0
        keys = jnp.concatenate([jnp.where(not_row0, k_prev[:BF16_ROWS], zero_bf), k_prev[BF16_ROWS:], kd[blk]], axis=0)
        vals = jnp.concatenate([jnp.where(not_row0, v_prev[:BF16_ROWS], zero_bf), v_prev[BF16_ROWS:], vd[blk]], axis=0)
        for kh in range(ATT_KVH):
            grp = slice(kh * LANES, (kh + 1) * LANES)
            k_dup, v_dup = keys[:, grp], vals[:, grp]
            c0 = kh * ATT_G * ATT_HD
            q2 = jnp.concatenate([q[blk, c0:c0 + LANES], q[blk, c0 + LANES:c0 + 2 * LANES]], axis=0)
            out = None
            for par in range(2):
                keep = lo_half if par == 0 else jnp.logical_not(lo_half)
                s = _dot_nt(q2, jnp.where(keep, k_dup, zero_bf)) + bias_ref[tbl, kh, par]
                p = jnp.exp2(s - jnp.max(s, axis=-1, keepdims=True))
                inv = 1.0 / jnp.sum(p, axis=-1, keepdims=True)
                pv = _dot(p.astype(BF16), jnp.where(keep, v_dup, zero_bf)) * inv
                out = pv if out is None else out + pv
            att_ref[blk, c0:c0 + LANES] = out[:w].astype(BF16)
            att_ref[blk, c0 + LANES:c0 + 2 * LANES] = out[w:].astype(BF16)

    kprev_ref[...] = kd[rows - w:]
    vprev_ref[...] = vd[rows - w:]
    mix = _dot(att_ref[...], wo_ref[...].astype(BF16)) + bo_ref[...]
    o_ref[0] = _layer_norm(DEEPNORM_ALPHA * x + mix, lng_ref[...], lnb_ref[...])


def _swa_layer(x, slopes, sinks, wkv, bkv, wq, bq, wo, bo, lng, lnb):
    bsz, seq, d = x.shape
    rows = SWA_ROWS
    smem = pl.BlockSpec(memory_space=pltpu.SMEM)
    return pl.pallas_call(
        _swa_layer_kernel,
        grid=(bsz, seq // rows),
        in_specs=[
            smem, smem,
            pl.BlockSpec((1, rows, d), lambda b, i: (b, i, 0)),
            _const_spec(wkv.shape), _const_spec(bkv.shape), _const_spec(wq.shape, 0), _const_spec(bq.shape),
            _const_spec(wo.shape, 0), _const_spec(bo.shape), _const_spec(lng.shape), _const_spec(lnb.shape),
        ],
        out_specs=pl.BlockSpec((1, rows, d), lambda b, i: (b, i, 0)),
        out_shape=jax.ShapeDtypeStruct(x.shape, F32),
        scratch_shapes=[pltpu.VMEM((WINDOW, ATT_KVH * LANES), BF16), pltpu.VMEM((WINDOW, ATT_KVH * LANES), BF16),
                        pltpu.VMEM((rows, ATT_QH * ATT_HD), BF16),
                        pltpu.VMEM((2, ATT_KVH, 2, 2 * WINDOW, 2 * WINDOW), F32)],
        compiler_params=pltpu.CompilerParams(dimension_semantics=("arbitrary", "arbitrary"),
                                             vmem_limit_bytes=VMEM_LIMIT_BYTES),
        name="swa_layer",
    )(slopes, sinks, x, wkv, bkv, wq, bq, wo, bo, lng, lnb)


def _dup_heads(w):
    lead = w.shape[:-1]
    w4 = w.reshape(lead + (ATT_KVH, 1, ATT_HD))
    return jnp.broadcast_to(w4, lead + (ATT_KVH, 2, ATT_HD)).reshape(lead + (ATT_KVH * 2 * ATT_HD,))


def kernel(x, p, a_w_in, a_lower_bound, a_norm_gain, a_w_out, kv_w, kv_b, b_w_q, b_b_q, b_sinks, b_w_out, b_b_out,
           ffn_w_gate_up, ffn_w_down, ple_w_up, ple_w_gate, ple_b_gate, ln_gain, ln_bias):
    bsz, seq, d = x.shape
    row = lambda v: v.reshape(1, -1).astype(F32)
    lower_bounds = jnp.cumsum(jax.nn.softmax(a_lower_bound.astype(F32), axis=0), axis=0)
    p3d = p.reshape(DEPTH, bsz * seq, PLE_DIM)

    def ffn_ple(i, h):
        out = _ffn_ple(i, h.reshape(bsz * seq, d), p3d, ffn_w_gate_up, ffn_w_down, ple_w_up, ple_w_gate,
                       row(ple_b_gate[i]), ln_gain[i, 1:3], ln_bias[i, 1:3])
        return out.reshape(bsz, seq, d)

    hgrn_args = (x, a_w_in, row(lower_bounds[0]), row(a_norm_gain[0]), a_w_out, row(ln_gain[0, 0]), row(ln_bias[0, 0]))
    h = lax.cond(jnp.min(lower_bounds[0]) >= HG_TWO_SIDED_MIN_LB,
                 functools.partial(_hgrn_layer, True), functools.partial(_hgrn_layer, False), *hgrn_args)
    h = ffn_ple(0, h)

    kdim = ATT_KVH * ATT_HD
    wkv = jnp.concatenate([_dup_heads(kv_w[:, :kdim]), _dup_heads(kv_w[:, kdim:])], axis=-1)
    bkv = row(jnp.concatenate([_dup_heads(kv_b[:kdim]), _dup_heads(kv_b[kdim:])], axis=-1))
    slopes = jnp.exp2(-8.0 * jnp.arange(1, ATT_QH + 1, dtype=F32) / ATT_QH)
    h = _swa_layer(h, slopes, b_sinks[0].astype(F32), wkv, bkv, b_w_q, row(b_b_q[0]),
                   b_w_out, row(b_b_out[0]), row(ln_gain[1, 0]), row(ln_bias[1, 0]))
    h = ffn_ple(1, h)
    return h
```

```python
import functools

import jax
import jax.numpy as jnp
from jax import lax
from jax.experimental import pallas as pl
from jax.experimental.pallas import tpu as pltpu

F32 = jnp.float32
BF16 = jnp.bfloat16

D_MODEL = 1024
DEPTH = 2
HG_DK = 128
HG_HEADS = 8
HG_WIDTH = HG_HEADS * HG_DK
ATT_HD = 64
ATT_QH = 16
ATT_KVH = 4
ATT_G = ATT_QH // ATT_KVH
WINDOW = 128
FFN_HIDDEN = 2816
PLE_DIM = 256
DEEPNORM_ALPHA = (2.0 * DEPTH) ** 0.25
LN_EPS = 1e-5
RMS_EPS = 1e-6

LANES = 128
SUBLANES = 8
BF16_ROWS = 16
LOG2E = 1.4426950408889634
VMEM_LIMIT_BYTES = 56 * 1024 * 1024

HG_CHUNK = 128
HG_ROWS = 512
HG_PROJ_CHUNKS = 2
HG_LEVELS = (64, 32, 16, 8, 4, 2, 1)
HG_BLOCK = BF16_ROWS
HG_TWO_SIDED_MIN_LB = 2.0 ** -6
FFN_ROWS = 256
SWA_ROWS = 256


def _dot(a, b):
    return jnp.dot(a, b, preferred_element_type=F32)


def _dot_nt(a, b):
    return lax.dot_general(a, b, (((1,), (1,)), ((), ())), preferred_element_type=F32)


def _dot_tn(a, b):
    return lax.dot_general(a, b, (((0,), (0,)), ((), ())), preferred_element_type=F32)


def _sigmoid(x):
    return 1.0 / (1.0 + jnp.exp(-x))


def _schedule_after(x, dep):
    corner = jnp.where(dep > jnp.inf, dep, x[:SUBLANES, :LANES])
    return jnp.concatenate([jnp.concatenate([corner, x[:SUBLANES, LANES:]], axis=1), x[SUBLANES:]], axis=0)


def _layer_norm(x, g, b):
    mu = jnp.mean(x, axis=-1, keepdims=True)
    xc = x - mu
    var = jnp.mean(xc * xc, axis=-1, keepdims=True)
    return xc * lax.rsqrt(var + LN_EPS) * g + b


def _hgrn_level_exponent(n, b, lf, row, sign):
    c, width = b.shape
    if n >= 4:
        b3 = b.reshape(c // (2 * n), 2 * n, width)
        d = (b3 - b3[:, n - 1:n, :]).reshape(c, width)
        return d * sign
    if n == 2:
        p = row & 3
        nxt = pltpu.roll(lf, c - 1, axis=0)
        prv = pltpu.roll(lf, 1, axis=0)
        zero = jnp.zeros_like(lf)
        return (jnp.where(p == 0, nxt, zero) + jnp.where(p >= 2, lf, zero)
                + jnp.where(p == 3, prv, zero))
    return jnp.where((row & 1) == 1, lf, jnp.zeros_like(lf))


def _hgrn_level_operands(n, q_bf, k_bf, b, lf, row, signs):
    c = b.shape[0]
    if n < BF16_ROWS:
        e = jnp.exp2(_hgrn_level_exponent(n, b, lf, row, signs.get(n)).astype(BF16))
        return q_bf * e, k_bf * e
    q_parts, k_parts = [], []
    for g in range(c // (2 * n)):
        lo = slice(2 * n * g, 2 * n * g + n)
        up = slice(2 * n * g + n, 2 * n * (g + 1))
        ref_row = b[2 * n * g + n - 1:2 * n * g + n, :]
        q_parts.append(q_bf[up] * jnp.exp2((b[up] - ref_row).astype(BF16)))
        k_parts += [k_bf[lo] * jnp.exp2((ref_row - b[lo]).astype(BF16)), k_bf[up]]
    return jnp.concatenate(q_parts, axis=0), jnp.concatenate(k_parts, axis=0)


def _hgrn_merge_level(n, tiles, s_n, lvl_tiles):
    level = n.bit_length() - 1

    def merge(t, s_tile):
        old = jnp.zeros_like(s_tile) if tiles[t] is None else tiles[t]
        tiles[t] = jnp.where(lvl_tiles[t] == level, s_tile, old)

    if n >= BF16_ROWS:
        per = n // SUBLANES
        for i in range(s_n.shape[0] // SUBLANES):
            merge(2 * per * (i // per) + per + i % per, s_n[i * SUBLANES:(i + 1) * SUBLANES])
    else:
        for t in range(len(tiles)):
            if n == SUBLANES and t % 2 == 0:
                continue
            merge(t, s_n[t * SUBLANES:(t + 1) * SUBLANES])


def _hgrn_layer_kernel(two_sided, x_ref, w_in_ref, lb_ref, gain_ref, w_out_ref, lng_ref, lnb_ref,
                       o_ref, st_ref, y_ref):
    c = HG_CHUNK
    levels = tuple(n for n in HG_LEVELS if n >= HG_BLOCK) if two_sided else HG_LEVELS

    @pl.when(pl.program_id(1) == 0)
    def _():
        st_ref[...] = jnp.zeros_like(st_ref)

    w_in = w_in_ref[...].astype(BF16)
    w_out = w_out_ref[...].astype(BF16)
    lb = lb_ref[...]
    one_m_lb = 1.0 - lb
    gain = gain_ref[...]

    rr = lax.broadcasted_iota(jnp.int32, (c, c), 0)
    cc = lax.broadcasted_iota(jnp.int32, (c, c), 1)
    tri = (rr >= cc).astype(BF16)
    xr = rr ^ cc
    lvl = jnp.full((c, c), -1, jnp.int32)
    for j in range(len(HG_LEVELS)):
        lvl = lvl + (xr >= (1 << j)).astype(jnp.int32)
    lvl = jnp.where(rr > cc, lvl, -1)
    lvl_tiles = [lvl[t * SUBLANES:(t + 1) * SUBLANES] for t in range(c // SUBLANES)]
    in_block = ((rr // HG_BLOCK) == (cc // HG_BLOCK)) & (rr >= cc)
    block_tiles = [in_block[t * SUBLANES:(t + 1) * SUBLANES] for t in range(c // SUBLANES)]
    row = lax.broadcasted_iota(jnp.int32, (c, HG_WIDTH), 0)
    signs = {n: jnp.where(((row // n) & 1) == 1, 1.0, -1.0).astype(F32) for n in HG_LEVELS if 4 <= n < BF16_ROWS}
    heads = [slice(h * HG_DK, (h + 1) * HG_DK) for h in range(HG_HEADS)]

    xs, projs = {}, {}

    def project_group(gi, after=None):
        m = HG_PROJ_CHUNKS
        xg = x_ref[0, m * gi * c:m * (gi + 1) * c, :]
        lhs = xg if after is None else _schedule_after(xg, after[c - SUBLANES:, :LANES])
        proj_g = _dot(lhs.astype(BF16), w_in)
        for k in range(m):
            xs[m * gi + k], projs[m * gi + k] = xg[k * c:(k + 1) * c], proj_g[k * c:(k + 1) * c]

    def gates(proj):
        gated = one_m_lb * _sigmoid(proj[:, HG_WIDTH:2 * HG_WIDTH])
        log_f = jnp.log2(lb + gated)
        k_all = one_m_lb - gated
        q_raw = proj[:, :HG_WIDTH]
        q_all = q_raw * _sigmoid(q_raw) * (HG_DK ** -0.5)
        v_all = proj[:, 2 * HG_WIDTH:3 * HG_WIDTH]
        g_raw = proj[:, 3 * HG_WIDTH:]
        g_all = g_raw * _sigmoid(g_raw)
        hi = log_f.astype(BF16)
        lo = (log_f - hi.astype(F32)).astype(BF16)
        b_all = _dot(tri, hi) + _dot(tri, lo)
        return q_all, k_all, log_f, b_all, v_all, g_all

    def level_operands(vals):
        q_all, k_all, log_f, b_all, v_all, _ = vals
        q_bf, k_bf, v_bf = q_all.astype(BF16), k_all.astype(BF16), v_all.astype(BF16)
        operands = {n: _hgrn_level_operands(n, q_bf, k_bf, b_all, log_f, row, signs) for n in levels}
        b_last = b_all[c - 1:c, :]
        q_dec = q_bf * jnp.exp2(b_all.astype(BF16))
        k_dec = k_bf * jnp.exp2((b_last - b_all).astype(BF16))
        if two_sided:
            b3 = b_all.reshape(c // HG_BLOCK, HG_BLOCK, HG_WIDTH)
            ref = jnp.concatenate([jnp.zeros((1, 1, HG_WIDTH), F32), b3[:-1, HG_BLOCK - 1:, :]], axis=0)
            rel = (b3 - ref).reshape(c, HG_WIDTH)
            within = (q_bf * jnp.exp2(rel).astype(BF16), k_bf * jnp.exp2(-rel).astype(BF16))
        else:
            within = q_all * k_all
        return operands, q_dec, k_dec, v_bf, jnp.exp2(b_last), within

    def scores_and_readout(vals, ops):
        v_all = vals[4]
        operands, q_dec, k_dec, v_bf, st_decay, within = ops
        tiles = [[None] * (c // SUBLANES) for _ in heads]
        for n in levels:
            q_n, k_n = operands[n]
            for h, sl in enumerate(heads):
                _hgrn_merge_level(n, tiles[h], _dot_nt(q_n[:, sl], k_n[:, sl]), lvl_tiles)
        if two_sided:
            for h, sl in enumerate(heads):
                s_blk = _dot_nt(within[0][:, sl], within[1][:, sl])
                for t, mask in enumerate(block_tiles):
                    s_tile = s_blk[t * SUBLANES:(t + 1) * SUBLANES]
                    old = jnp.zeros_like(s_tile) if tiles[h][t] is None else tiles[h][t]
                    tiles[h][t] = jnp.where(mask, s_tile, old)
        outs = []
        for h, sl in enumerate(heads):
            st = st_ref[h]
            o_h = _dot(jnp.concatenate(tiles[h], axis=0).astype(BF16), v_bf[:, sl])
            o_h = o_h + _dot_nt(q_dec[:, sl], st.astype(BF16))
            st_ref[h] = st * st_decay[:, sl] + _dot_tn(v_bf[:, sl], k_dec[:, sl])
            if not two_sided:
                o_h = o_h + jnp.sum(within[:, sl], axis=-1, keepdims=True) * v_all[:, sl]
            outs.append(o_h)
        return outs

    def finish(ci, x, g_all, outs):
        rows_c = slice(ci * c, (ci + 1) * c)
        for h, sl in enumerate(heads):
            ms = jnp.mean(outs[h] * outs[h], axis=-1, keepdims=True)
            y_ref[rows_c, sl] = (outs[h] * lax.rsqrt(ms + RMS_EPS) * gain * g_all[:, sl]).astype(BF16)
        mix = _dot(y_ref[rows_c, :], w_out)
        o_ref[0, rows_c, :] = _layer_norm(DEEPNORM_ALPHA * x + mix, lng_ref[...], lnb_ref[...])

    n_chunks = x_ref.shape[1] // c
    project_group(0)
    vals = gates(projs[0])
    for ci in range(n_chunks):
        ops = level_operands(vals)
        if ci % HG_PROJ_CHUNKS == 0 and ci + HG_PROJ_CHUNKS < n_chunks:
            project_group(ci // HG_PROJ_CHUNKS + 1, after=vals[3])
        outs = scores_and_readout(vals, ops)
        g_cur = vals[5]
        if ci + 1 < n_chunks:
            vals = gates(projs[ci + 1])
        finish(ci, xs[ci], g_cur, outs)


def _const_spec(shape, layer=None):
    if layer is None:
        zeros = (0,) * len(shape)
        return pl.BlockSpec(shape, lambda *_: zeros, pipeline_mode=pl.Buffered(1))
    zeros = (0,) * (len(shape) - 1)
    return pl.BlockSpec((None,) + tuple(shape[1:]), lambda *_: (layer,) + zeros, pipeline_mode=pl.Buffered(1))


def _hgrn_layer(two_sided, x, w_in, lb, gain, w_out, lng, lnb):
    bsz, seq, d = x.shape
    c = HG_ROWS
    return pl.pallas_call(
        functools.partial(_hgrn_layer_kernel, two_sided),
        grid=(bsz, seq // c),
        in_specs=[
            pl.BlockSpec((1, c, d), lambda b, i: (b, i, 0)),
            _const_spec(w_in.shape, 0), _const_spec(lb.shape), _const_spec(gain.shape),
            _const_spec(w_out.shape, 0), _const_spec(lng.shape), _const_spec(lnb.shape),
        ],
        out_specs=pl.BlockSpec((1, c, d), lambda b, i: (b, i, 0)),
        out_shape=jax.ShapeDtypeStruct(x.shape, F32),
        scratch_shapes=[pltpu.VMEM((HG_HEADS, HG_DK, HG_DK), F32), pltpu.VMEM((c, HG_WIDTH), BF16)],
        compiler_params=pltpu.CompilerParams(dimension_semantics=("arbitrary", "arbitrary"),
                                             vmem_limit_bytes=VMEM_LIMIT_BYTES),
        name="hgrn_layer",
    )(x, w_in, lb, gain, w_out, lng, lnb)


def _ffn_ple_kernel(x_ref, p_ref, wgu_ref, wd_ref, wpu_ref, wpg_ref, bpg_ref, lng_ref, lnb_ref, o_ref):
    x = x_ref[...]
    gu = _dot(x.astype(BF16), wgu_ref[...].astype(BF16))
    gate, up = gu[:, :FFN_HIDDEN], gu[:, FFN_HIDDEN:]
    hid = (gate * _sigmoid(gate) * up).astype(BF16)
    x = _layer_norm(DEEPNORM_ALPHA * x + _dot(hid, wd_ref[...].astype(BF16)), lng_ref[0:1, :], lnb_ref[0:1, :])
    ple_gate = _sigmoid(_dot(x.astype(BF16), wpg_ref[...].astype(BF16)) + bpg_ref[...])
    ple_up = _dot(p_ref[...].astype(BF16), wpu_ref[...].astype(BF16))
    o_ref[...] = _layer_norm(DEEPNORM_ALPHA * x + ple_gate * ple_up, lng_ref[1:2, :], lnb_ref[1:2, :])


def _ffn_ple(layer, x2d, p3d, wgu, wd, wpu, wpg, bpg, lng, lnb):
    t, d = x2d.shape
    rows = FFN_ROWS
    return pl.pallas_call(
        _ffn_ple_kernel,
        grid=(t // rows,),
        in_specs=[
            pl.BlockSpec((rows, d), lambda i: (i, 0)),
            pl.BlockSpec((None, rows, PLE_DIM), lambda i: (layer, i, 0)),
            _const_spec(wgu.shape, layer), _const_spec(wd.shape, layer), _const_spec(wpu.shape, layer),
            _const_spec(wpg.shape, layer), _const_spec(bpg.shape), _const_spec(lng.shape), _const_spec(lnb.shape),
        ],
        out_specs=pl.BlockSpec((rows, d), lambda i: (i, 0)),
        out_shape=jax.ShapeDtypeStruct(x2d.shape, F32),
        compiler_params=pltpu.CompilerParams(dimension_semantics=("arbitrary",),
                                             vmem_limit_bytes=VMEM_LIMIT_BYTES),
        name="ffn_ple",
    )(x2d, p3d, wgu, wd, wpu, wpg, bpg, lng, lnb)


def _swa_layer_kernel(slopes_ref, sinks_ref, x_ref, wkv_ref, bkv_ref, wq_ref, bq_ref, wo_ref, bo_ref,
                      lng_ref, lnb_ref, o_ref, kprev_ref, vprev_ref, att_ref, bias_ref):
    w = WINDOW
    rows = x_ref.shape[1]
    first = pl.program_id(1) == 0

    @pl.when((pl.program_id(0) == 0) & first)
    def _():
        qi = lax.broadcasted_iota(jnp.int32, (w, 2 * w), 0)
        si = lax.broadcasted_iota(jnp.int32, (w, 2 * w), 1)
        dist = qi - si + w
        band = (dist >= 0) & (dist < w)
        dist_f = dist.astype(F32)
        prev_keys = (si >= 1) & (si < w)
        for kh in range(ATT_KVH):
            for par in range(2):
                for r in range(2):
                    head = kh * ATT_G + 2 * r + par
                    bias = jnp.where(band, (slopes_ref[head] * dist_f) * (-LOG2E), -jnp.inf)
                    bias = jnp.where(si == 0, jnp.full((w, 2 * w), sinks_ref[head], F32) * LOG2E, bias)
                    bias_ref[0, kh, par, r * w:(r + 1) * w, :] = bias
                    bias_ref[1, kh, par, r * w:(r + 1) * w, :] = jnp.where(prev_keys, -jnp.inf, bias)

    @pl.when(first)
    def _():
        kprev_ref[...] = jnp.zeros_like(kprev_ref)
        vprev_ref[...] = jnp.zeros_like(vprev_ref)

    x = x_ref[0]
    xb = x.astype(BF16)
    kv = _dot(xb, wkv_ref[...].astype(BF16)) + bkv_ref[...]
    kd = kv[:, :ATT_KVH * LANES].astype(BF16)
    vd = kv[:, ATT_KVH * LANES:].astype(BF16)
    q = ((_dot(xb, wq_ref[...].astype(BF16)) + bq_ref[...]) * (ATT_HD ** -0.5 * LOG2E)).astype(BF16)

    lo_half = lax.broadcasted_iota(jnp.int32, (1, LANES), 1) < ATT_HD
    not_row0 = lax.broadcasted_iota(jnp.int32, (BF16_ROWS, 1), 0) > 0
    zero_bf = jnp.zeros((), BF16)
    first_idx = first.astype(jnp.int32)

    for j in range(rows // w):
        blk = slice(j * w, (j + 1) * w)
        if j == 0:
            k_prev, v_prev, tbl = kprev_ref[...], vprev_ref[...], first_idx
        else:
            prev = slice((j - 1) * w, j * w)
            k_prev, v_prev, tbl = kd[prev], vd[prev], 0
        keys = jnp.concatenate([jnp.where(not_row0, k_prev[:BF16_ROWS], zero_bf), k_prev[BF16_ROWS:], kd[blk]], axis=0)
        vals = jnp.concatenate([jnp.where(not_row0, v_prev[:BF16_ROWS], zero_bf), v_prev[BF16_ROWS:], vd[blk]], axis=0)
        for kh in range(ATT_KVH):
            grp = slice(kh * LANES, (kh + 1) * LANES)
            k_dup, v_dup = keys[:, grp], vals[:, grp]
            c0 = kh * ATT_G * ATT_HD
            q2 = jnp.concatenate([q[blk, c0:c0 + LANES], q[blk, c0 + LANES:c0 + 2 * LANES]], axis=0)
            out = None
            for par in range(2):
                keep = lo_half if par == 0 else jnp.logical_not(lo_half)
                s = _dot_nt(q2, jnp.where(keep, k_dup, zero_bf)) + bias_ref[tbl, kh, par]
                p = jnp.exp2(s - jnp.max(s, axis=-1, keepdims=True))
                inv = 1.0 / jnp.sum(p, axis=-1, keepdims=True)
                pv = _dot(p.astype(BF16), jnp.where(keep, v_dup, zero_bf)) * inv
                out = pv if out is None else out + pv
            att_ref[blk, c0:c0 + LANES] = out[:w].astype(BF16)
            att_ref[blk, c0 + LANES:c0 + 2 * LANES] = out[w:].astype(BF16)

    kprev_ref[...] = kd[rows - w:]
    vprev_ref[...] = vd[rows - w:]
    mix = _dot(att_ref[...], wo_ref[...].astype(BF16)) + bo_ref[...]
    o_ref[0] = _layer_norm(DEEPNORM_ALPHA * x + mix, lng_ref[...], lnb_ref[...])


def _swa_layer(x, slopes, sinks, wkv, bkv, wq, bq, wo, bo, lng, lnb):
    bsz, seq, d = x.shape
    rows = SWA_ROWS
    smem = pl.BlockSpec(memory_space=pltpu.SMEM)
    return pl.pallas_call(
        _swa_layer_kernel,
        grid=(bsz, seq // rows),
        in_specs=[
            smem, smem,
            pl.BlockSpec((1, rows, d), lambda b, i: (b, i, 0)),
            _const_spec(wkv.shape), _const_spec(bkv.shape), _const_spec(wq.shape, 0), _const_spec(bq.shape),
            _const_spec(wo.shape, 0), _const_spec(bo.shape), _const_spec(lng.shape), _const_spec(lnb.shape),
        ],
        out_specs=pl.BlockSpec((1, rows, d), lambda b, i: (b, i, 0)),
        out_shape=jax.ShapeDtypeStruct(x.shape, F32),
        scratch_shapes=[pltpu.VMEM((WINDOW, ATT_KVH * LANES), BF16), pltpu.VMEM((WINDOW, ATT_KVH * LANES), BF16),
                        pltpu.VMEM((rows, ATT_QH * ATT_HD), BF16),
                        pltpu.VMEM((2, ATT_KVH, 2, 2 * WINDOW, 2 * WINDOW), F32)],
        compiler_params=pltpu.CompilerParams(dimension_semantics=("arbitrary", "arbitrary"),
                                             vmem_limit_bytes=VMEM_LIMIT_BYTES),
        name="swa_layer",
    )(slopes, sinks, x, wkv, bkv, wq, bq, wo, bo, lng, lnb)


def _dup_heads(w):
    lead = w.shape[:-1]
    w4 = w.reshape(lead + (ATT_KVH, 1, ATT_HD))
    return jnp.broadcast_to(w4, lead + (ATT_KVH, 2, ATT_HD)).reshape(lead + (ATT_KVH * 2 * ATT_HD,))


def kernel(x, p, a_w_in, a_lower_bound, a_norm_gain, a_w_out, kv_w, kv_b, b_w_q, b_b_q, b_sinks, b_w_out, b_b_out,
           ffn_w_gate_up, ffn_w_down, ple_w_up, ple_w_gate, ple_b_gate, ln_gain, ln_bias):
    bsz, seq, d = x.shape
    row = lambda v: v.reshape(1, -1).astype(F32)
    lower_bounds = jnp.cumsum(jax.nn.softmax(a_lower_bound.astype(F32), axis=0), axis=0)
    p3d = p.reshape(DEPTH, bsz * seq, PLE_DIM)

    def ffn_ple(i, h):
        out = _ffn_ple(i, h.reshape(bsz * seq, d), p3d, ffn_w_gate_up, ffn_w_down, ple_w_up, ple_w_gate,
                       row(ple_b_gate[i]), ln_gain[i, 1:3], ln_bias[i, 1:3])
        return out.reshape(bsz, seq, d)

    hgrn_args = (x, a_w_in, row(lower_bounds[0]), row(a_norm_gain[0]), a_w_out, row(ln_gain[0, 0]), row(ln_bias[0, 0]))
    h = lax.cond(jnp.min(lower_bounds[0]) >= HG_TWO_SIDED_MIN_LB,
                 functools.partial(_hgrn_layer, True), functools.partial(_hgrn_layer, False), *hgrn_args)
    h = ffn_ple(0, h)

    kdim = ATT_KVH * ATT_HD
    wkv = jnp.concatenate([_dup_heads(kv_w[:, :kdim]), _dup_heads(kv_w[:, kdim:])], axis=-1)
    bkv = row(jnp.concatenate([_dup_heads(kv_b[:kdim]), _dup_heads(kv_b[kdim:])], axis=-1))
    slopes = jnp.exp2(-8.0 * jnp.arange(1, ATT_QH + 1, dtype=F32) / ATT_QH)
    h = _swa_layer(h, slopes, b_sinks[0].astype(F32), wkv, bkv, b_w_q, row(b_b_q[0]),
                   b_w_out, row(b_b_out[0]), row(ln_gain[1, 0]), row(ln_bias[1, 0]))
    h = ffn_ple(1, h)
    return h
```

```python
import functools

import jax
import jax.numpy as jnp
from jax import lax
from jax.experimental import pallas as pl
from jax.experimental.pallas import tpu as pltpu

F32 = jnp.float32
BF16 = jnp.bfloat16

D_MODEL = 1024
DEPTH = 2
HG_DK = 128
HG_HEADS = 8
HG_WIDTH = HG_HEADS * HG_DK
ATT_HD = 64
ATT_QH = 16
ATT_KVH = 4
ATT_G = ATT_QH // ATT_KVH
WINDOW = 128
FFN_HIDDEN = 2816
PLE_DIM = 256
DEEPNORM_ALPHA = (2.0 * DEPTH) ** 0.25
LN_EPS = 1e-5
RMS_EPS = 1e-6

LANES = 128
SUBLANES = 8
BF16_ROWS = 16
LOG2E = 1.4426950408889634
VMEM_LIMIT_BYTES = 56 * 1024 * 1024

HG_CHUNK = 128
HG_ROWS = 512
HG_PROJ_CHUNKS = 2
HG_LEVELS = (64, 32, 16, 8, 4, 2, 1)
HG_BLOCK = BF16_ROWS
HG_TWO_SIDED_MIN_LB = 2.0 ** -6
FFN_ROWS = 512
MXU_WIDTH = 256
FFN_HIDDEN_GROUPS = ((0, 6 * MXU_WIDTH), (6 * MXU_WIDTH, FFN_HIDDEN))
SWA_ROWS = 256


def _dot(a, b):
    return jnp.dot(a, b, preferred_element_type=F32)


def _dot_nt(a, b):
    return lax.dot_general(a, b, (((1,), (1,)), ((), ())), preferred_element_type=F32)


def _dot_tn(a, b):
    return lax.dot_general(a, b, (((0,), (0,)), ((), ())), preferred_element_type=F32)


def _sigmoid(x):
    return 1.0 / (1.0 + jnp.exp(-x))


def _schedule_after(x, dep):
    corner = jnp.where(dep > jnp.inf, dep, x[:SUBLANES, :LANES])
    return jnp.concatenate([jnp.concatenate([corner, x[:SUBLANES, LANES:]], axis=1), x[SUBLANES:]], axis=0)


def _layer_norm(x, g, b):
    mu = jnp.mean(x, axis=-1, keepdims=True)
    xc = x - mu
    var = jnp.mean(xc * xc, axis=-1, keepdims=True)
    return xc * lax.rsqrt(var + LN_EPS) * g + b


def _hgrn_level_exponent(n, b, lf, row, sign):
    c, width = b.shape
    if n >= 4:
        b3 = b.reshape(c // (2 * n), 2 * n, width)
        d = (b3 - b3[:, n - 1:n, :]).reshape(c, width)
        return d * sign
    if n == 2:
        p = row & 3
        nxt = pltpu.roll(lf, c - 1, axis=0)
        prv = pltpu.roll(lf, 1, axis=0)
        zero = jnp.zeros_like(lf)
        return (jnp.where(p == 0, nxt, zero) + jnp.where(p >= 2, lf, zero)
                + jnp.where(p == 3, prv, zero))
    return jnp.where((row & 1) == 1, lf, jnp.zeros_like(lf))


def _hgrn_level_operands(n, q_bf, k_bf, b, lf, row, signs):
    c = b.shape[0]
    if n < BF16_ROWS:
        e = jnp.exp2(_hgrn_level_exponent(n, b, lf, row, signs.get(n)).astype(BF16))
        return q_bf * e, k_bf * e
    q_parts, k_parts = [], []
    for g in range(c // (2 * n)):
        lo = slice(2 * n * g, 2 * n * g + n)
        up = slice(2 * n * g + n, 2 * n * (g + 1))
        ref_row = b[2 * n * g + n - 1:2 * n * g + n, :]
        q_parts.append(q_bf[up] * jnp.exp2((b[up] - ref_row).astype(BF16)))
        k_parts += [k_bf[lo] * jnp.exp2((ref_row - b[lo]).astype(BF16)), k_bf[up]]
    return jnp.concatenate(q_parts, axis=0), jnp.concatenate(k_parts, axis=0)


def _hgrn_merge_level(n, tiles, s_n, lvl_tiles):
    level = n.bit_length() - 1

    def merge(t, s_tile):
        old = jnp.zeros_like(s_tile) if tiles[t] is None else tiles[t]
        tiles[t] = jnp.where(lvl_tiles[t] == level, s_tile, old)

    if n >= BF16_ROWS:
        per = n // SUBLANES
        for i in range(s_n.shape[0] // SUBLANES):
            merge(2 * per * (i // per) + per + i % per, s_n[i * SUBLANES:(i + 1) * SUBLANES])
    else:
        for t in range(len(tiles)):
            if n == SUBLANES and t % 2 == 0:
                continue
            merge(t, s_n[t * SUBLANES:(t + 1) * SUBLANES])


def _hgrn_layer_kernel(two_sided, x_ref, w_in_ref, lb_ref, gain_ref, w_out_ref, lng_ref, lnb_ref,
                       o_ref, st_ref, y_ref):
    c = HG_CHUNK
    levels = tuple(n for n in HG_LEVELS if n >= HG_BLOCK) if two_sided else HG_LEVELS

    @pl.when(pl.program_id(1) == 0)
    def _():
        st_ref[...] = jnp.zeros_like(st_ref)

    w_in = w_in_ref[...].astype(BF16)
    w_out = w_out_ref[...].astype(BF16)
    lb = lb_ref[...]
    one_m_lb = 1.0 - lb
    gain = gain_ref[...]

    rr = lax.broadcasted_iota(jnp.int32, (c, c), 0)
    cc = lax.broadcasted_iota(jnp.int32, (c, c), 1)
    tri = (rr >= cc).astype(BF16)
    xr = rr ^ cc
    lvl = jnp.full((c, c), -1, jnp.int32)
    for j in range(len(HG_LEVELS)):
        lvl = lvl + (xr >= (1 << j)).astype(jnp.int32)
    lvl = jnp.where(rr > cc, lvl, -1)
    lvl_tiles = [lvl[t * SUBLANES:(t + 1) * SUBLANES] for t in range(c // SUBLANES)]
    in_block = ((rr // HG_BLOCK) == (cc // HG_BLOCK)) & (rr >= cc)
    block_tiles = [in_block[t * SUBLANES:(t + 1) * SUBLANES] for t in range(c // SUBLANES)]
    row = lax.broadcasted_iota(jnp.int32, (c, HG_WIDTH), 0)
    signs = {n: jnp.where(((row // n) & 1) == 1, 1.0, -1.0).astype(F32) for n in HG_LEVELS if 4 <= n < BF16_ROWS}
    heads = [slice(h * HG_DK, (h + 1) * HG_DK) for h in range(HG_HEADS)]

    xs, projs = {}, {}

    def project_group(gi, after=None):
        m = HG_PROJ_CHUNKS
        xg = x_ref[0, m * gi * c:m * (gi + 1) * c, :]
        lhs = xg if after is None else _schedule_after(xg, after[c - SUBLANES:, :LANES])
        proj_g = _dot(lhs.astype(BF16), w_in)
        for k in range(m):
            xs[m * gi + k], projs[m * gi + k] = xg[k * c:(k + 1) * c], proj_g[k * c:(k + 1) * c]

    def gates(proj):
        gated = one_m_lb * _sigmoid(proj[:, HG_WIDTH:2 * HG_WIDTH])
        log_f = jnp.log2(lb + gated)
        k_all = one_m_lb - gated
        q_raw = proj[:, :HG_WIDTH]
        q_all = q_raw * _sigmoid(q_raw) * (HG_DK ** -0.5)
        v_all = proj[:, 2 * HG_WIDTH:3 * HG_WIDTH]
        g_raw = proj[:, 3 * HG_WIDTH:]
        g_all = g_raw * _sigmoid(g_raw)
        hi = log_f.astype(BF16)
        lo = (log_f - hi.astype(F32)).astype(BF16)
        b_all = _dot(tri, hi) + _dot(tri, lo)
        return q_all, k_all, log_f, b_all, v_all, g_all

    def level_operands(vals):
        q_all, k_all, log_f, b_all, v_all, _ = vals
        q_bf, k_bf, v_bf = q_all.astype(BF16), k_all.astype(BF16), v_all.astype(BF16)
        operands = {n: _hgrn_level_operands(n, q_bf, k_bf, b_all, log_f, row, signs) for n in levels}
        b_last = b_all[c - 1:c, :]
        q_dec = q_bf * jnp.exp2(b_all.astype(BF16))
        k_dec = k_bf * jnp.exp2((b_last - b_all).astype(BF16))
        if two_sided:
            b3 = b_all.reshape(c // HG_BLOCK, HG_BLOCK, HG_WIDTH)
            ref = jnp.concatenate([jnp.zeros((1, 1, HG_WIDTH), F32), b3[:-1, HG_BLOCK - 1:, :]], axis=0)
            rel = (b3 - ref).reshape(c, HG_WIDTH)
            within = (q_bf * jnp.exp2(rel).astype(BF16), k_bf * jnp.exp2(-rel).astype(BF16))
        else:
            within = q_all * k_all
        return operands, q_dec, k_dec, v_bf, jnp.exp2(b_last), within

    def scores_and_readout(vals, ops):
        v_all = vals[4]
        operands, q_dec, k_dec, v_bf, st_decay, within = ops
        tiles = [[None] * (c // SUBLANES) for _ in heads]
        for n in levels:
            q_n, k_n = operands[n]
            for h, sl in enumerate(heads):
                _hgrn_merge_level(n, tiles[h], _dot_nt(q_n[:, sl], k_n[:, sl]), lvl_tiles)
        if two_sided:
            for h, sl in enumerate(heads):
                s_blk = _dot_nt(within[0][:, sl], within[1][:, sl])
                for t, mask in enumerate(block_tiles):
                    s_tile = s_blk[t * SUBLANES:(t + 1) * SUBLANES]
                    old = jnp.zeros_like(s_tile) if tiles[h][t] is None else tiles[h][t]
                    tiles[h][t] = jnp.where(mask, s_tile, old)
        outs = []
        for h, sl in enumerate(heads):
            st = st_ref[h]
            o_h = _dot(jnp.concatenate(tiles[h], axis=0).astype(BF16), v_bf[:, sl])
            o_h = o_h + _dot_nt(q_dec[:, sl], st.astype(BF16))
            st_ref[h] = st * st_decay[:, sl] + _dot_tn(v_bf[:, sl], k_dec[:, sl])
            if not two_sided:
                o_h = o_h + jnp.sum(within[:, sl], axis=-1, keepdims=True) * v_all[:, sl]
            outs.append(o_h)
        return outs

    def finish(ci, x, g_all, outs):
        rows_c = slice(ci * c, (ci + 1) * c)
        for h, sl in enumerate(heads):
            ms = jnp.mean(outs[h] * outs[h], axis=-1, keepdims=True)
            y_ref[rows_c, sl] = (outs[h] * lax.rsqrt(ms + RMS_EPS) * gain * g_all[:, sl]).astype(BF16)
        mix = _dot(y_ref[rows_c, :], w_out)
        o_ref[0, rows_c, :] = _layer_norm(DEEPNORM_ALPHA * x + mix, lng_ref[...], lnb_ref[...])

    n_chunks = x_ref.shape[1] // c
    project_group(0)
    vals = gates(projs[0])
    for ci in range(n_chunks):
        ops = level_operands(vals)
        if ci % HG_PROJ_CHUNKS == 0 and ci + HG_PROJ_CHUNKS < n_chunks:
            project_group(ci // HG_PROJ_CHUNKS + 1, after=vals[3])
        outs = scores_and_readout(vals, ops)
        g_cur = vals[5]
        if ci + 1 < n_chunks:
            vals = gates(projs[ci + 1])
        finish(ci, xs[ci], g_cur, outs)


def _const_spec(shape, layer=None):
    if layer is None:
        zeros = (0,) * len(shape)
        return pl.BlockSpec(shape, lambda *_: zeros, pipeline_mode=pl.Buffered(1))
    zeros = (0,) * (len(shape) - 1)
    return pl.BlockSpec((None,) + tuple(shape[1:]), lambda *_: (layer,) + zeros, pipeline_mode=pl.Buffered(1))


def _hgrn_layer(two_sided, x, w_in, lb, gain, w_out, lng, lnb):
    bsz, seq, d = x.shape
    c = HG_ROWS
    return pl.pallas_call(
        functools.partial(_hgrn_layer_kernel, two_sided),
        grid=(bsz, seq // c),
        in_specs=[
            pl.BlockSpec((1, c, d), lambda b, i: (b, i, 0)),
            _const_spec(w_in.shape, 0), _const_spec(lb.shape), _const_spec(gain.shape),
            _const_spec(w_out.shape, 0), _const_spec(lng.shape), _const_spec(lnb.shape),
        ],
        out_specs=pl.BlockSpec((1, c, d), lambda b, i: (b, i, 0)),
        out_shape=jax.ShapeDtypeStruct(x.shape, F32),
        scratch_shapes=[pltpu.VMEM((HG_HEADS, HG_DK, HG_DK), F32), pltpu.VMEM((c, HG_WIDTH), BF16)],
        compiler_params=pltpu.CompilerParams(dimension_semantics=("arbitrary", "arbitrary"),
                                             vmem_limit_bytes=VMEM_LIMIT_BYTES),
        name="hgrn_layer",
    )(x, w_in, lb, gain, w_out, lng, lnb)


def _ffn_ple_kernel(x_ref, p_ref, wgu_ref, wd_ref, wpu_ref, wpg_ref, bpg_ref, lng_ref, lnb_ref, o_ref):
    x = x_ref[...]
    xb = x.astype(BF16)
    down = None
    for lo, hi in FFN_HIDDEN_GROUPS:
        cols, up_cols = slice(lo, hi), slice(FFN_HIDDEN + lo, FFN_HIDDEN + hi)
        gate = _dot(xb, wgu_ref[:, cols].astype(BF16))
        up = _dot(xb, wgu_ref[:, up_cols].astype(BF16))
        part = _dot((gate * _sigmoid(gate) * up).astype(BF16), wd_ref[cols, :].astype(BF16))
        down = part if down is None else down + part
    x = _layer_norm(DEEPNORM_ALPHA * x + down, lng_ref[0:1, :], lnb_ref[0:1, :])
    ple_gate = _sigmoid(_dot(x.astype(BF16), wpg_ref[...].astype(BF16)) + bpg_ref[...])
    ple_up = _dot(p_ref[...].astype(BF16), wpu_ref[...].astype(BF16))
    o_ref[...] = _layer_norm(DEEPNORM_ALPHA * x + ple_gate * ple_up, lng_ref[1:2, :], lnb_ref[1:2, :])


def _ffn_ple(layer, x2d, p3d, wgu, wd, wpu, wpg, bpg, lng, lnb):
    t, d = x2d.shape
    rows = FFN_ROWS
    return pl.pallas_call(
        _ffn_ple_kernel,
        grid=(t // rows,),
        in_specs=[
            pl.BlockSpec((rows, d), lambda i: (i, 0)),
            pl.BlockSpec((None, rows, PLE_DIM), lambda i: (layer, i, 0)),
            _const_spec(wgu.shape, layer), _const_spec(wd.shape, layer), _const_spec(wpu.shape, layer),
            _const_spec(wpg.shape, layer), _const_spec(bpg.shape), _const_spec(lng.shape), _const_spec(lnb.shape),
        ],
        out_specs=pl.BlockSpec((rows, d), lambda i: (i, 0)),
        out_shape=jax.ShapeDtypeStruct(x2d.shape, F32),
        compiler_params=pltpu.CompilerParams(dimension_semantics=("arbitrary",),
                                             vmem_limit_bytes=VMEM_LIMIT_BYTES),
        name="ffn_ple",
    )(x2d, p3d, wgu, wd, wpu, wpg, bpg, lng, lnb)


def _swa_layer_kernel(slopes_ref, sinks_ref, x_ref, wkv_ref, bkv_ref, wq_ref, bq_ref, wo_ref, bo_ref,
                      lng_ref, lnb_ref, o_ref, kprev_ref, vprev_ref, att_ref, bias_ref):
    w = WINDOW
    rows = x_ref.shape[1]
    first = pl.program_id(1) == 0

    @pl.when((pl.program_id(0) == 0) & first)
    def _():
        qi = lax.broadcasted_iota(jnp.int32, (w, 2 * w), 0)
        si = lax.broadcasted_iota(jnp.int32, (w, 2 * w), 1)
        dist = qi - si + w
        band = (dist >= 0) & (dist < w)
        dist_f = dist.astype(F32)
        prev_keys = (si >= 1) & (si < w)
        for kh in range(ATT_KVH):
            for par in range(2):
                for r in range(2):
                    head = kh * ATT_G + 2 * r + par
                    bias = jnp.where(band, (slopes_ref[head] * dist_f) * (-LOG2E), -jnp.inf)
                    bias = jnp.where(si == 0, jnp.full((w, 2 * w), sinks_ref[head], F32) * LOG2E, bias)
                    bias_ref[0, kh, par, r * w:(r + 1) * w, :] = bias
                    bias_ref[1, kh, par, r * w:(r + 1) * w, :] = jnp.where(prev_keys, -jnp.inf, bias)

    @pl.when(first)
    def _():
        kprev_ref[...] = jnp.zeros_like(kprev_ref)
        vprev_ref[...] = jnp.zeros_like(vprev_ref)

    x = x_ref[0]
    xb = x.astype(BF16)
    kv = _dot(xb, wkv_ref[...].astype(BF16)) + bkv_ref[...]
    kd = kv[:, :ATT_KVH * LANES].astype(BF16)
    vd = kv[:, ATT_KVH * LANES:].astype(BF16)
    q = ((_dot(xb, wq_ref[...].astype(BF16)) + bq_ref[...]) * (ATT_HD ** -0.5 * LOG2E)).astype(BF16)

    lo_half = lax.broadcasted_iota(jnp.int32, (1, LANES), 1) < ATT_HD
    not_row0 = lax.broadcasted_iota(jnp.int32, (BF16_ROWS, 1), 0) > 0
    zero_bf = jnp.zeros((), BF16)
    first_idx = first.astype(jnp.int32)

    for j in range(rows // w):
        blk = slice(j * w, (j + 1) * w)
        if j == 0:
            k_prev, v_prev, tbl = kprev_ref[...], vprev_ref[...], first_idx
        else:
            prev = slice((j - 1) * w, j * w)
            k_prev, v_prev, tbl = kd[prev], vd[prev], 0
        keys = jnp.concatenate([jnp.where(not_row0, k_prev[:BF16_ROWS], zero_bf), k_prev[BF16_ROWS:], kd[blk]], axis=0)
        vals = jnp.concatenate([jnp.where(not_row0, v_prev[:BF16_ROWS], zero_bf), v_prev[BF16_ROWS:], vd[blk]], axis=0)
        for kh in range(ATT_KVH):
            grp = slice(kh * LANES, (kh + 1) * LANES)
            k_dup, v_dup = keys[:, grp], vals[:, grp]
            c0 = kh * ATT_G * ATT_HD
            q2 = jnp.concatenate([q[blk, c0:c0 + LANES], q[blk, c0 + LANES:c0 + 2 * LANES]], axis=0)
            out = None
            for par in range(2):
                keep = lo_half if par == 0 else jnp.logical_not(lo_half)
                s = _dot_nt(q2, jnp.where(keep, k_dup, zero_bf)) + bias_ref[tbl, kh, par]
                p = jnp.exp2(s - jnp.max(s, axis=-1, keepdims=True))
                inv = 1.0 / jnp.sum(p, axis=-1, keepdims=True)
                pv = _dot(p.astype(BF16), jnp.where(keep, v_dup, zero_bf)) * inv
                out = pv if out is None else out + pv
            att_ref[blk, c0:c0 + LANES] = out[:w].astype(BF16)
            att_ref[blk, c0 + LANES:c0 + 2 * LANES] = out[w:].astype(BF16)

    kprev_ref[...] = kd[rows - w:]
    vprev_ref[...] = vd[rows - w:]
    mix = _dot(att_ref[...], wo_ref[...].astype(BF16)) + bo_ref[...]
    o_ref[0] = _layer_norm(DEEPNORM_ALPHA * x + mix, lng_ref[...], lnb_ref[...])


def _swa_layer(x, slopes, sinks, wkv, bkv, wq, bq, wo, bo, lng, lnb):
    bsz, seq, d = x.shape
    rows = SWA_ROWS
    smem = pl.BlockSpec(memory_space=pltpu.SMEM)
    return pl.pallas_call(
        _swa_layer_kernel,
        grid=(bsz, seq // rows),
        in_specs=[
            smem, smem,
            pl.BlockSpec((1, rows, d), lambda b, i: (b, i, 0)),
            _const_spec(wkv.shape), _const_spec(bkv.shape), _const_spec(wq.shape, 0), _const_spec(bq.shape),
            _const_spec(wo.shape, 0), _const_spec(bo.shape), _const_spec(lng.shape), _const_spec(lnb.shape),
        ],
        out_specs=pl.BlockSpec((1, rows, d), lambda b, i: (b, i, 0)),
        out_shape=jax.ShapeDtypeStruct(x.shape, F32),
        scratch_shapes=[pltpu.VMEM((WINDOW, ATT_KVH * LANES), BF16), pltpu.VMEM((WINDOW, ATT_KVH * LANES), BF16),
                        pltpu.VMEM((rows, ATT_QH * ATT_HD), BF16),
                        pltpu.VMEM((2, ATT_KVH, 2, 2 * WINDOW, 2 * WINDOW), F32)],
        compiler_params=pltpu.CompilerParams(dimension_semantics=("arbitrary", "arbitrary"),
                                             vmem_limit_bytes=VMEM_LIMIT_BYTES),
        name="swa_layer",
    )(slopes, sinks, x, wkv, bkv, wq, bq, wo, bo, lng, lnb)


def _dup_heads(w):
    lead = w.shape[:-1]
    w4 = w.reshape(lead + (ATT_KVH, 1, ATT_HD))
    return jnp.broadcast_to(w4, lead + (ATT_KVH, 2, ATT_HD)).reshape(lead + (ATT_KVH * 2 * ATT_HD,))


def kernel(x, p, a_w_in, a_lower_bound, a_norm_gain, a_w_out, kv_w, kv_b, b_w_q, b_b_q, b_sinks, b_w_out, b_b_out,
           ffn_w_gate_up, ffn_w_down, ple_w_up, ple_w_gate, ple_b_gate, ln_gain, ln_bias):
    bsz, seq, d = x.shape
    row = lambda v: v.reshape(1, -1).astype(F32)
    lower_bounds = jnp.cumsum(jax.nn.softmax(a_lower_bound.astype(F32), axis=0), axis=0)
    p3d = p.reshape(DEPTH, bsz * seq, PLE_DIM)

    def ffn_ple(i, h):
        out = _ffn_ple(i, h.reshape(bsz * seq, d), p3d, ffn_w_gate_up, ffn_w_down, ple_w_up, ple_w_gate,
                       row(ple_b_gate[i]), ln_gain[i, 1:3], ln_bias[i, 1:3])
        return out.reshape(bsz, seq, d)

    hgrn_args = (x, a_w_in, row(lower_bounds[0]), row(a_norm_gain[0]), a_w_out, row(ln_gain[0, 0]), row(ln_bias[0, 0]))
    h = lax.cond(jnp.min(lower_bounds[0]) >= HG_TWO_SIDED_MIN_LB,
                 functools.partial(_hgrn_layer, True), functools.partial(_hgrn_layer, False), *hgrn_args)
    h = ffn_ple(0, h)

    kdim = ATT_KVH * ATT_HD
    wkv = jnp.concatenate([_dup_heads(kv_w[:, :kdim]), _dup_heads(kv_w[:, kdim:])], axis=-1)
    bkv = row(jnp.concatenate([_dup_heads(kv_b[:kdim]), _dup_heads(kv_b[kdim:])], axis=-1))
    slopes = jnp.exp2(-8.0 * jnp.arange(1, ATT_QH + 1, dtype=F32) / ATT_QH)
    h = _swa_layer(h, slopes, b_sinks[0].astype(F32), wkv, bkv, b_w_q, row(b_b_q[0]),
                   b_w_out, row(b_b_out[0]), row(ln_gain[1, 0]), row(ln_bias[1, 0]))
    h = ffn_ple(1, h)
    return h
```

```python
import functools

import jax
import jax.numpy as jnp
from jax import lax
from jax.experimental import pallas as pl
from jax.experimental.pallas import tpu as pltpu

F32 = jnp.float32
BF16 = jnp.bfloat16

D_MODEL = 1024
DEPTH = 2
HG_DK = 128
HG_HEADS = 8
HG_WIDTH = HG_HEADS * HG_DK
ATT_HD = 64
ATT_QH = 16
ATT_KVH = 4
ATT_G = ATT_QH // ATT_KVH
WINDOW = 128
FFN_HIDDEN = 2816
PLE_DIM = 256
DEEPNORM_ALPHA = (2.0 * DEPTH) ** 0.25
LN_EPS = 1e-5
RMS_EPS = 1e-6

LANES = 128
SUBLANES = 8
BF16_ROWS = 16
LOG2E = 1.4426950408889634
VMEM_LIMIT_BYTES = 56 * 1024 * 1024

HG_CHUNK = 128
HG_ROWS = 512
HG_PROJ_CHUNKS = 2
HG_LEVELS = (64, 32, 16, 8, 4, 2, 1)
HG_BLOCK = BF16_ROWS
HG_TWO_SIDED_MIN_LB = 2.0 ** -6
FFN_ROWS = 512
MXU_WIDTH = 256
FFN_HIDDEN_GROUPS = ((0, 6 * MXU_WIDTH), (6 * MXU_WIDTH, FFN_HIDDEN))
FFN_TAIL_ROWS = ((0, FFN_ROWS // 2), (FFN_ROWS // 2, FFN_ROWS))
SWA_ROWS = 256


def _dot(a, b):
    return jnp.dot(a, b, preferred_element_type=F32)


def _dot_nt(a, b):
    return lax.dot_general(a, b, (((1,), (1,)), ((), ())), preferred_element_type=F32)


def _dot_tn(a, b):
    return lax.dot_general(a, b, (((0,), (0,)), ((), ())), preferred_element_type=F32)


def _sigmoid(x):
    return 1.0 / (1.0 + jnp.exp(-x))


def _schedule_after(x, dep):
    corner = jnp.where(dep > jnp.inf, dep, x[:SUBLANES, :LANES])
    return jnp.concatenate([jnp.concatenate([corner, x[:SUBLANES, LANES:]], axis=1), x[SUBLANES:]], axis=0)


def _layer_norm(x, g, b):
    mu = jnp.mean(x, axis=-1, keepdims=True)
    xc = x - mu
    var = jnp.mean(xc * xc, axis=-1, keepdims=True)
    return xc * lax.rsqrt(var + LN_EPS) * g + b


def _hgrn_level_exponent(n, b, lf, row, sign):
    c, width = b.shape
    if n >= 4:
        b3 = b.reshape(c // (2 * n), 2 * n, width)
        d = (b3 - b3[:, n - 1:n, :]).reshape(c, width)
        return d * sign
    if n == 2:
        p = row & 3
        nxt = pltpu.roll(lf, c - 1, axis=0)
        prv = pltpu.roll(lf, 1, axis=0)
        zero = jnp.zeros_like(lf)
        return (jnp.where(p == 0, nxt, zero) + jnp.where(p >= 2, lf, zero)
                + jnp.where(p == 3, prv, zero))
    return jnp.where((row & 1) == 1, lf, jnp.zeros_like(lf))


def _hgrn_level_operands(n, q_bf, k_bf, b, lf, row, signs):
    c = b.shape[0]
    if n < BF16_ROWS:
        e = jnp.exp2(_hgrn_level_exponent(n, b, lf, row, signs.get(n)).astype(BF16))
        return q_bf * e, k_bf * e
    q_parts, k_parts = [], []
    for g in range(c // (2 * n)):
        lo = slice(2 * n * g, 2 * n * g + n)
        up = slice(2 * n * g + n, 2 * n * (g + 1))
        ref_row = b[2 * n * g + n - 1:2 * n * g + n, :]
        q_parts.append(q_bf[up] * jnp.exp2((b[up] - ref_row).astype(BF16)))
        k_parts += [k_bf[lo] * jnp.exp2((ref_row - b[lo]).astype(BF16)), k_bf[up]]
    return jnp.concatenate(q_parts, axis=0), jnp.concatenate(k_parts, axis=0)


def _hgrn_merge_level(n, tiles, s_n, lvl_tiles):
    level = n.bit_length() - 1

    def merge(t, s_tile):
        old = jnp.zeros_like(s_tile) if tiles[t] is None else tiles[t]
        tiles[t] = jnp.where(lvl_tiles[t] == level, s_tile, old)

    if n >= BF16_ROWS:
        per = n // SUBLANES
        for i in range(s_n.shape[0] // SUBLANES):
            merge(2 * per * (i // per) + per + i % per, s_n[i * SUBLANES:(i + 1) * SUBLANES])
    else:
        for t in range(len(tiles)):
            if n == SUBLANES and t % 2 == 0:
                continue
            merge(t, s_n[t * SUBLANES:(t + 1) * SUBLANES])


def _hgrn_layer_kernel(two_sided, x_ref, w_in_ref, lb_ref, gain_ref, w_out_ref, lng_ref, lnb_ref,
                       o_ref, st_ref, y_ref):
    c = HG_CHUNK
    levels = tuple(n for n in HG_LEVELS if n >= HG_BLOCK) if two_sided else HG_LEVELS

    @pl.when(pl.program_id(1) == 0)
    def _():
        st_ref[...] = jnp.zeros_like(st_ref)

    w_in = w_in_ref[...].astype(BF16)
    w_out = w_out_ref[...].astype(BF16)
    lb = lb_ref[...]
    one_m_lb = 1.0 - lb
    gain = gain_ref[...]

    rr = lax.broadcasted_iota(jnp.int32, (c, c), 0)
    cc = lax.broadcasted_iota(jnp.int32, (c, c), 1)
    tri = (rr >= cc).astype(BF16)
    xr = rr ^ cc
    lvl = jnp.full((c, c), -1, jnp.int32)
    for j in range(len(HG_LEVELS)):
        lvl = lvl + (xr >= (1 << j)).astype(jnp.int32)
    lvl = jnp.where(rr > cc, lvl, -1)
    lvl_tiles = [lvl[t * SUBLANES:(t + 1) * SUBLANES] for t in range(c // SUBLANES)]
    in_block = ((rr // HG_BLOCK) == (cc // HG_BLOCK)) & (rr >= cc)
    block_tiles = [in_block[t * SUBLANES:(t + 1) * SUBLANES] for t in range(c // SUBLANES)]
    row = lax.broadcasted_iota(jnp.int32, (c, HG_WIDTH), 0)
    signs = {n: jnp.where(((row // n) & 1) == 1, 1.0, -1.0).astype(F32) for n in HG_LEVELS if 4 <= n < BF16_ROWS}
    heads = [slice(h * HG_DK, (h + 1) * HG_DK) for h in range(HG_HEADS)]

    xs, projs = {}, {}

    def project_group(gi, after=None):
        m = HG_PROJ_CHUNKS
        xg = x_ref[0, m * gi * c:m * (gi + 1) * c, :]
        lhs = xg if after is None else _schedule_after(xg, after[c - SUBLANES:, :LANES])
        proj_g = _dot(lhs.astype(BF16), w_in)
        for k in range(m):
            xs[m * gi + k], projs[m * gi + k] = xg[k * c:(k + 1) * c], proj_g[k * c:(k + 1) * c]

    def gates(proj):
        gated = one_m_lb * _sigmoid(proj[:, HG_WIDTH:2 * HG_WIDTH])
        log_f = jnp.log2(lb + gated)
        k_all = one_m_lb - gated
        q_raw = proj[:, :HG_WIDTH]
        q_all = q_raw * _sigmoid(q_raw) * (HG_DK ** -0.5)
        v_all = proj[:, 2 * HG_WIDTH:3 * HG_WIDTH]
        g_raw = proj[:, 3 * HG_WIDTH:]
        g_all = g_raw * _sigmoid(g_raw)
        hi = log_f.astype(BF16)
        lo = (log_f - hi.astype(F32)).astype(BF16)
        b_all = _dot(tri, hi) + _dot(tri, lo)
        return q_all, k_all, log_f, b_all, v_all, g_all

    def level_operands(vals):
        q_all, k_all, log_f, b_all, v_all, _ = vals
        q_bf, k_bf, v_bf = q_all.astype(BF16), k_all.astype(BF16), v_all.astype(BF16)
        operands = {n: _hgrn_level_operands(n, q_bf, k_bf, b_all, log_f, row, signs) for n in levels}
        b_last = b_all[c - 1:c, :]
        q_dec = q_bf * jnp.exp2(b_all.astype(BF16))
        k_dec = k_bf * jnp.exp2((b_last - b_all).astype(BF16))
        if two_sided:
            b3 = b_all.reshape(c // HG_BLOCK, HG_BLOCK, HG_WIDTH)
            ref = jnp.concatenate([jnp.zeros((1, 1, HG_WIDTH), F32), b3[:-1, HG_BLOCK - 1:, :]], axis=0)
            rel = (b3 - ref).reshape(c, HG_WIDTH)
            within = (q_bf * jnp.exp2(rel).astype(BF16), k_bf * jnp.exp2(-rel).astype(BF16))
        else:
            within = q_all * k_all
        return operands, q_dec, k_dec, v_bf, jnp.exp2(b_last), within

    def scores_and_readout(vals, ops):
        v_all = vals[4]
        operands, q_dec, k_dec, v_bf, st_decay, within = ops
        tiles = [[None] * (c // SUBLANES) for _ in heads]
        for n in levels:
            q_n, k_n = operands[n]
            for h, sl in enumerate(heads):
                _hgrn_merge_level(n, tiles[h], _dot_nt(q_n[:, sl], k_n[:, sl]), lvl_tiles)
        if two_sided:
            for h, sl in enumerate(heads):
                s_blk = _dot_nt(within[0][:, sl], within[1][:, sl])
                for t, mask in enumerate(block_tiles):
                    s_tile = s_blk[t * SUBLANES:(t + 1) * SUBLANES]
                    old = jnp.zeros_like(s_tile) if tiles[h][t] is None else tiles[h][t]
                    tiles[h][t] = jnp.where(mask, s_tile, old)
        outs = []
        for h, sl in enumerate(heads):
            st = st_ref[h]
            o_h = _dot(jnp.concatenate(tiles[h], axis=0).astype(BF16), v_bf[:, sl])
            o_h = o_h + _dot_nt(q_dec[:, sl], st.astype(BF16))
            st_ref[h] = st * st_decay[:, sl] + _dot_tn(v_bf[:, sl], k_dec[:, sl])
            if not two_sided:
                o_h = o_h + jnp.sum(within[:, sl], axis=-1, keepdims=True) * v_all[:, sl]
            outs.append(o_h)
        return outs

    def finish(ci, x, g_all, outs):
        rows_c = slice(ci * c, (ci + 1) * c)
        for h, sl in enumerate(heads):
            ms = jnp.mean(outs[h] * outs[h], axis=-1, keepdims=True)
            y_ref[rows_c, sl] = (outs[h] * lax.rsqrt(ms + RMS_EPS) * gain * g_all[:, sl]).astype(BF16)
        mix = _dot(y_ref[rows_c, :], w_out)
        o_ref[0, rows_c, :] = _layer_norm(DEEPNORM_ALPHA * x + mix, lng_ref[...], lnb_ref[...])

    n_chunks = x_ref.shape[1] // c
    project_group(0)
    vals = gates(projs[0])
    for ci in range(n_chunks):
        ops = level_operands(vals)
        if ci % HG_PROJ_CHUNKS == 0 and ci + HG_PROJ_CHUNKS < n_chunks:
            project_group(ci // HG_PROJ_CHUNKS + 1, after=vals[3])
        outs = scores_and_readout(vals, ops)
        g_cur = vals[5]
        if ci + 1 < n_chunks:
            vals = gates(projs[ci + 1])
        finish(ci, xs[ci], g_cur, outs)


def _const_spec(shape, layer=None):
    if layer is None:
        zeros = (0,) * len(shape)
        return pl.BlockSpec(shape, lambda *_: zeros, pipeline_mode=pl.Buffered(1))
    zeros = (0,) * (len(shape) - 1)
    return pl.BlockSpec((None,) + tuple(shape[1:]), lambda *_: (layer,) + zeros, pipeline_mode=pl.Buffered(1))


def _hgrn_layer(two_sided, x, w_in, lb, gain, w_out, lng, lnb):
    bsz, seq, d = x.shape
    c = HG_ROWS
    return pl.pallas_call(
        functools.partial(_hgrn_layer_kernel, two_sided),
        grid=(bsz, seq // c),
        in_specs=[
            pl.BlockSpec((1, c, d), lambda b, i: (b, i, 0)),
            _const_spec(w_in.shape, 0), _const_spec(lb.shape), _const_spec(gain.shape),
            _const_spec(w_out.shape, 0), _const_spec(lng.shape), _const_spec(lnb.shape),
        ],
        out_specs=pl.BlockSpec((1, c, d), lambda b, i: (b, i, 0)),
        out_shape=jax.ShapeDtypeStruct(x.shape, F32),
        scratch_shapes=[pltpu.VMEM((HG_HEADS, HG_DK, HG_DK), F32), pltpu.VMEM((c, HG_WIDTH), BF16)],
        compiler_params=pltpu.CompilerParams(dimension_semantics=("arbitrary", "arbitrary"),
                                             vmem_limit_bytes=VMEM_LIMIT_BYTES),
        name="hgrn_layer",
    )(x, w_in, lb, gain, w_out, lng, lnb)


def _ffn_ple_kernel(x_ref, p_ref, wgu_ref, wd_ref, wpu_ref, wpg_ref, bpg_ref, lng_ref, lnb_ref, o_ref):
    x = x_ref[...]
    xb = x.astype(BF16)
    halves = [slice(lo, hi) for lo, hi in FFN_TAIL_ROWS]
    downs = None
    for gi, (lo, hi) in enumerate(FFN_HIDDEN_GROUPS):
        cols, up_cols = slice(lo, hi), slice(FFN_HIDDEN + lo, FFN_HIDDEN + hi)
        gate = _dot(xb, wgu_ref[:, cols].astype(BF16))
        up = _dot(xb, wgu_ref[:, up_cols].astype(BF16))
        hid = (gate * _sigmoid(gate) * up).astype(BF16)
        wd = wd_ref[cols, :].astype(BF16)
        if gi + 1 < len(FFN_HIDDEN_GROUPS):
            part = _dot(hid, wd)
            parts = [part[r] for r in halves]
        else:
            parts = [_dot(hid[r], wd) for r in halves]
        downs = parts if downs is None else [a + b for a, b in zip(downs, parts)]
    ple_up = _dot(p_ref[...].astype(BF16), wpu_ref[...].astype(BF16))
    wpg = wpg_ref[...].astype(BF16)
    for i, r in enumerate(halves):
        x1 = _layer_norm(DEEPNORM_ALPHA * x[r] + downs[i], lng_ref[0:1, :], lnb_ref[0:1, :])
        lhs = _schedule_after(x1, downs[i + 1][-SUBLANES:, :LANES]) if i + 1 < len(halves) else x1
        ple_gate = _sigmoid(_dot(lhs.astype(BF16), wpg) + bpg_ref[...])
        o_ref[r, :] = _layer_norm(DEEPNORM_ALPHA * x1 + ple_gate * ple_up[r], lng_ref[1:2, :], lnb_ref[1:2, :])


def _ffn_ple(layer, x2d, p3d, wgu, wd, wpu, wpg, bpg, lng, lnb):
    t, d = x2d.shape
    rows = FFN_ROWS
    return pl.pallas_call(
        _ffn_ple_kernel,
        grid=(t // rows,),
        in_specs=[
            pl.BlockSpec((rows, d), lambda i: (i, 0)),
            pl.BlockSpec((None, rows, PLE_DIM), lambda i: (layer, i, 0)),
            _const_spec(wgu.shape, layer), _const_spec(wd.shape, layer), _const_spec(wpu.shape, layer),
            _const_spec(wpg.shape, layer), _const_spec(bpg.shape), _const_spec(lng.shape), _const_spec(lnb.shape),
        ],
        out_specs=pl.BlockSpec((rows, d), lambda i: (i, 0)),
        out_shape=jax.ShapeDtypeStruct(x2d.shape, F32),
        compiler_params=pltpu.CompilerParams(dimension_semantics=("arbitrary",),
                                             vmem_limit_bytes=VMEM_LIMIT_BYTES),
        name="ffn_ple",
    )(x2d, p3d, wgu, wd, wpu, wpg, bpg, lng, lnb)


def _swa_layer_kernel(slopes_ref, sinks_ref, x_ref, wkv_ref, bkv_ref, wq_ref, bq_ref, wo_ref, bo_ref,
                      lng_ref, lnb_ref, o_ref, kprev_ref, vprev_ref, att_ref, bias_ref):
    w = WINDOW
    rows = x_ref.shape[1]
    first = pl.program_id(1) == 0

    @pl.when((pl.program_id(0) == 0) & first)
    def _():
        qi = lax.broadcasted_iota(jnp.int32, (w, 2 * w), 0)
        si = lax.broadcasted_iota(jnp.int32, (w, 2 * w), 1)
        dist = qi - si + w
        band = (dist >= 0) & (dist < w)
        dist_f = dist.astype(F32)
        prev_keys = (si >= 1) & (si < w)
        for kh in range(ATT_KVH):
            for par in range(2):
                for r in range(2):
                    head = kh * ATT_G + 2 * r + par
                    bias = jnp.where(band, (slopes_ref[head] * dist_f) * (-LOG2E), -jnp.inf)
                    bias = jnp.where(si == 0, jnp.full((w, 2 * w), sinks_ref[head], F32) * LOG2E, bias)
                    bias_ref[0, kh, par, r * w:(r + 1) * w, :] = bias
                    bias_ref[1, kh, par, r * w:(r + 1) * w, :] = jnp.where(prev_keys, -jnp.inf, bias)

    @pl.when(first)
    def _():
        kprev_ref[...] = jnp.zeros_like(kprev_ref)
        vprev_ref[...] = jnp.zeros_like(vprev_ref)

    x = x_ref[0]
    xb = x.astype(BF16)
    kv = _dot(xb, wkv_ref[...].astype(BF16)) + bkv_ref[...]
    kd = kv[:, :ATT_KVH * LANES].astype(BF16)
    vd = kv[:, ATT_KVH * LANES:].astype(BF16)
    q = ((_dot(xb, wq_ref[...].astype(BF16)) + bq_ref[...]) * (ATT_HD ** -0.5 * LOG2E)).astype(BF16)

    lo_half = lax.broadcasted_iota(jnp.int32, (1, LANES), 1) < ATT_HD
    not_row0 = lax.broadcasted_iota(jnp.int32, (BF16_ROWS, 1), 0) > 0
    zero_bf = jnp.zeros((), BF16)
    first_idx = first.astype(jnp.int32)

    for j in range(rows // w):
        blk = slice(j * w, (j + 1) * w)
        if j == 0:
            k_prev, v_prev, tbl = kprev_ref[...], vprev_ref[...], first_idx
        else:
            prev = slice((j - 1) * w, j * w)
            k_prev, v_prev, tbl = kd[prev], vd[prev], 0
        keys = jnp.concatenate([jnp.where(not_row0, k_prev[:BF16_ROWS], zero_bf), k_prev[BF16_ROWS:], kd[blk]], axis=0)
        vals = jnp.concatenate([jnp.where(not_row0, v_prev[:BF16_ROWS], zero_bf), v_prev[BF16_ROWS:], vd[blk]], axis=0)
        for kh in range(ATT_KVH):
            grp = slice(kh * LANES, (kh + 1) * LANES)
            k_dup, v_dup = keys[:, grp], vals[:, grp]
            c0 = kh * ATT_G * ATT_HD
            q2 = jnp.concatenate([q[blk, c0:c0 + LANES], q[blk, c0 + LANES:c0 + 2 * LANES]], axis=0)
            out = None
            for par in range(2):
                keep = lo_half if par == 0 else jnp.logical_not(lo_half)
                s = _dot_nt(q2, jnp.where(keep, k_dup, zero_bf)) + bias_ref[tbl, kh, par]
                p = jnp.exp2(s - jnp.max(s, axis=-1, keepdims=True))
                inv = 1.0 / jnp.sum(p, axis=-1, keepdims=True)
                pv = _dot(p.astype(BF16), jnp.where(keep, v_dup, zero_bf)) * inv
                out = pv if out is None else out + pv
            att_ref[blk, c0:c0 + LANES] = out[:w].astype(BF16)
            att_ref[blk, c0 + LANES:c0 + 2 * LANES] = out[w:].astype(BF16)

    kprev_ref[...] = kd[rows - w:]
    vprev_ref[...] = vd[rows - w:]
    mix = _dot(att_ref[...], wo_ref[...].astype(BF16)) + bo_ref[...]
    o_ref[0] = _layer_norm(DEEPNORM_ALPHA * x + mix, lng_ref[...], lnb_ref[...])


def _swa_layer(x, slopes, sinks, wkv, bkv, wq, bq, wo, bo, lng, lnb):
    bsz, seq, d = x.shape
    rows = SWA_ROWS
    smem = pl.BlockSpec(memory_space=pltpu.SMEM)
    return pl.pallas_call(
        _swa_layer_kernel,
        grid=(bsz, seq // rows),
        in_specs=[
            smem, smem,
            pl.BlockSpec((1, rows, d), lambda b, i: (b, i, 0)),
            _const_spec(wkv.shape), _const_spec(bkv.shape), _const_spec(wq.shape, 0), _const_spec(bq.shape),
            _const_spec(wo.shape, 0), _const_spec(bo.shape), _const_spec(lng.shape), _const_spec(lnb.shape),
        ],
        out_specs=pl.BlockSpec((1, rows, d), lambda b, i: (b, i, 0)),
        out_shape=jax.ShapeDtypeStruct(x.shape, F32),
        scratch_shapes=[pltpu.VMEM((WINDOW, ATT_KVH * LANES), BF16), pltpu.VMEM((WINDOW, ATT_KVH * LANES), BF16),
                        pltpu.VMEM((rows, ATT_QH * ATT_HD), BF16),
                        pltpu.VMEM((2, ATT_KVH, 2, 2 * WINDOW, 2 * WINDOW), F32)],
        compiler_params=pltpu.CompilerParams(dimension_semantics=("arbitrary", "arbitrary"),
                                             vmem_limit_bytes=VMEM_LIMIT_BYTES),
        name="swa_layer",
    )(slopes, sinks, x, wkv, bkv, wq, bq, wo, bo, lng, lnb)


def _dup_heads(w):
    lead = w.shape[:-1]
    w4 = w.reshape(lead + (ATT_KVH, 1, ATT_HD))
    return jnp.broadcast_to(w4, lead + (ATT_KVH, 2, ATT_HD)).reshape(lead + (ATT_KVH * 2 * ATT_HD,))


def kernel(x, p, a_w_in, a_lower_bound, a_norm_gain, a_w_out, kv_w, kv_b, b_w_q, b_b_q, b_sinks, b_w_out, b_b_out,
           ffn_w_gate_up, ffn_w_down, ple_w_up, ple_w_gate, ple_b_gate, ln_gain, ln_bias):
    bsz, seq, d = x.shape
    row = lambda v: v.reshape(1, -1).astype(F32)
    lower_bounds = jnp.cumsum(jax.nn.softmax(a_lower_bound.astype(F32), axis=0), axis=0)
    p3d = p.reshape(DEPTH, bsz * seq, PLE_DIM)

    def ffn_ple(i, h):
        out = _ffn_ple(i, h.reshape(bsz * seq, d), p3d, ffn_w_gate_up, ffn_w_down, ple_w_up, ple_w_gate,
                       row(ple_b_gate[i]), ln_gain[i, 1:3], ln_bias[i, 1:3])
        return out.reshape(bsz, seq, d)

    hgrn_args = (x, a_w_in, row(lower_bounds[0]), row(a_norm_gain[0]), a_w_out, row(ln_gain[0, 0]), row(ln_bias[0, 0]))
    h = lax.cond(jnp.min(lower_bounds[0]) >= HG_TWO_SIDED_MIN_LB,
                 functools.partial(_hgrn_layer, True), functools.partial(_hgrn_layer, False), *hgrn_args)
    h = ffn_ple(0, h)

    kdim = ATT_KVH * ATT_HD
    wkv = jnp.concatenate([_dup_heads(kv_w[:, :kdim]), _dup_heads(kv_w[:, kdim:])], axis=-1)
    bkv = row(jnp.concatenate([_dup_heads(kv_b[:kdim]), _dup_heads(kv_b[kdim:])], axis=-1))
    slopes = jnp.exp2(-8.0 * jnp.arange(1, ATT_QH + 1, dtype=F32) / ATT_QH)
    h = _swa_layer(h, slopes, b_sinks[0].astype(F32), wkv, bkv, b_w_q, row(b_b_q[0]),
                   b_w_out, row(b_b_out[0]), row(ln_gain[1, 0]), row(ln_bias[1, 0]))
    h = ffn_ple(1, h)
    return h
```

```python
import functools

import jax
import jax.numpy as jnp
from jax import lax
from jax.experimental import pallas as pl
from jax.experimental.pallas import tpu as pltpu

F32 = jnp.float32
BF16 = jnp.bfloat16

D_MODEL = 1024
DEPTH = 2
HG_DK = 128
HG_HEADS = 8
HG_WIDTH = HG_HEADS * HG_DK
ATT_HD = 64
ATT_QH = 16
ATT_KVH = 4
ATT_G = ATT_QH // ATT_KVH
WINDOW = 128
FFN_HIDDEN = 2816
PLE_DIM = 256
DEEPNORM_ALPHA = (2.0 * DEPTH) ** 0.25
LN_EPS = 1e-5
RMS_EPS = 1e-6

LANES = 128
SUBLANES = 8
BF16_ROWS = 16
LOG2E = 1.4426950408889634
VMEM_LIMIT_BYTES = 56 * 1024 * 1024

HG_CHUNK = 128
HG_ROWS = 512
HG_PROJ_CHUNKS = 2
HG_LEVELS = (64, 32, 16, 8, 4, 2, 1)
HG_BLOCK = BF16_ROWS
HG_TWO_SIDED_MIN_LB = 2.0 ** -6
FFN_ROWS = 512
MXU_WIDTH = 256
FFN_HIDDEN_GROUPS = ((0, 6 * MXU_WIDTH), (6 * MXU_WIDTH, FFN_HIDDEN))
FFN_TAIL_ROWS = ((0, FFN_ROWS // 2), (FFN_ROWS // 2, FFN_ROWS))
SWA_ROWS = 256


def _dot(a, b):
    return jnp.dot(a, b, preferred_element_type=F32)


def _dot_nt(a, b):
    return lax.dot_general(a, b, (((1,), (1,)), ((), ())), preferred_element_type=F32)


def _dot_tn(a, b):
    return lax.dot_general(a, b, (((0,), (0,)), ((), ())), preferred_element_type=F32)


def _sigmoid(x):
    return 1.0 / (1.0 + jnp.exp(-x))


def _schedule_after(x, dep):
    corner = jnp.where(dep > jnp.inf, dep, x[:SUBLANES, :LANES])
    return jnp.concatenate([jnp.concatenate([corner, x[:SUBLANES, LANES:]], axis=1), x[SUBLANES:]], axis=0)


def _layer_norm(x, g, b):
    mu = jnp.mean(x, axis=-1, keepdims=True)
    xc = x - mu
    var = jnp.mean(xc * xc, axis=-1, keepdims=True)
    return xc * lax.rsqrt(var + LN_EPS) * g + b


def _hgrn_level_exponent(n, b, lf, row, sign):
    c, width = b.shape
    if n >= 4:
        b3 = b.reshape(c // (2 * n), 2 * n, width)
        d = (b3 - b3[:, n - 1:n, :]).reshape(c, width)
        return d * sign
    if n == 2:
        p = row & 3
        nxt = pltpu.roll(lf, c - 1, axis=0)
        prv = pltpu.roll(lf, 1, axis=0)
        zero = jnp.zeros_like(lf)
        return (jnp.where(p == 0, nxt, zero) + jnp.where(p >= 2, lf, zero)
                + jnp.where(p == 3, prv, zero))
    return jnp.where((row & 1) == 1, lf, jnp.zeros_like(lf))


def _hgrn_level_operands(n, q_bf, k_bf, b, lf, row, signs):
    c = b.shape[0]
    if n < BF16_ROWS:
        e = jnp.exp2(_hgrn_level_exponent(n, b, lf, row, signs.get(n)).astype(BF16))
        return q_bf * e, k_bf * e
    q_parts, k_parts = [], []
    for g in range(c // (2 * n)):
        lo = slice(2 * n * g, 2 * n * g + n)
        up = slice(2 * n * g + n, 2 * n * (g + 1))
        ref_row = b[2 * n * g + n - 1:2 * n * g + n, :]
        q_parts.append(q_bf[up] * jnp.exp2((b[up] - ref_row).astype(BF16)))
        k_parts += [k_bf[lo] * jnp.exp2((ref_row - b[lo]).astype(BF16)), k_bf[up]]
    return jnp.concatenate(q_parts, axis=0), jnp.concatenate(k_parts, axis=0)


def _hgrn_merge_level(n, tiles, s_n, lvl_tiles):
    level = n.bit_length() - 1

    def merge(t, s_tile):
        old = jnp.zeros_like(s_tile) if tiles[t] is None else tiles[t]
        tiles[t] = jnp.where(lvl_tiles[t] == level, s_tile, old)

    if n >= BF16_ROWS:
        per = n // SUBLANES
        for i in range(s_n.shape[0] // SUBLANES):
            merge(2 * per * (i // per) + per + i % per, s_n[i * SUBLANES:(i + 1) * SUBLANES])
    else:
        for t in range(len(tiles)):
            if n == SUBLANES and t % 2 == 0:
                continue
            merge(t, s_n[t * SUBLANES:(t + 1) * SUBLANES])


def _hgrn_layer_kernel(two_sided, x_ref, w_in_ref, lb_ref, gain_ref, w_out_ref, lng_ref, lnb_ref,
                       o_ref, st_ref, y_ref):
    c = HG_CHUNK
    levels = tuple(n for n in HG_LEVELS if n >= HG_BLOCK) if two_sided else HG_LEVELS

    @pl.when(pl.program_id(1) == 0)
    def _():
        st_ref[...] = jnp.zeros_like(st_ref)

    w_in = w_in_ref[...].astype(BF16)
    w_out = w_out_ref[...].astype(BF16)
    lb = lb_ref[...]
    one_m_lb = 1.0 - lb
    gain = gain_ref[...]

    rr = lax.broadcasted_iota(jnp.int32, (c, c), 0)
    cc = lax.broadcasted_iota(jnp.int32, (c, c), 1)
    tri = (rr >= cc).astype(BF16)
    xr = rr ^ cc
    lvl = jnp.full((c, c), -1, jnp.int32)
    for j in range(len(HG_LEVELS)):
        lvl = lvl + (xr >= (1 << j)).astype(jnp.int32)
    lvl = jnp.where(rr > cc, lvl, -1)
    lvl_tiles = [lvl[t * SUBLANES:(t + 1) * SUBLANES] for t in range(c // SUBLANES)]
    in_block = ((rr // HG_BLOCK) == (cc // HG_BLOCK)) & (rr >= cc)
    block_tiles = [in_block[t * SUBLANES:(t + 1) * SUBLANES] for t in range(c // SUBLANES)]
    row = lax.broadcasted_iota(jnp.int32, (c, HG_WIDTH), 0)
    signs = {n: jnp.where(((row // n) & 1) == 1, 1.0, -1.0).astype(F32) for n in HG_LEVELS if 4 <= n < BF16_ROWS}
    heads = [slice(h * HG_DK, (h + 1) * HG_DK) for h in range(HG_HEADS)]

    xs, projs = {}, {}

    def project_group(gi, after=None):
        m = HG_PROJ_CHUNKS
        xg = x_ref[0, m * gi * c:m * (gi + 1) * c, :]
        lhs = xg if after is None else _schedule_after(xg, after[c - SUBLANES:, :LANES])
        proj_g = _dot(lhs.astype(BF16), w_in)
        for k in range(m):
            xs[m * gi + k], projs[m * gi + k] = xg[k * c:(k + 1) * c], proj_g[k * c:(k + 1) * c]

    def gates(proj):
        gated = one_m_lb * _sigmoid(proj[:, HG_WIDTH:2 * HG_WIDTH])
        log_f = jnp.log2(lb + gated)
        k_all = one_m_lb - gated
        q_raw = proj[:, :HG_WIDTH]
        q_all = q_raw * _sigmoid(q_raw) * (HG_DK ** -0.5)
        v_all = proj[:, 2 * HG_WIDTH:3 * HG_WIDTH]
        g_raw = proj[:, 3 * HG_WIDTH:]
        g_all = g_raw * _sigmoid(g_raw)
        hi = log_f.astype(BF16)
        lo = (log_f - hi.astype(F32)).astype(BF16)
        b_all = _dot(tri, hi) + _dot(tri, lo)
        return q_all, k_all, log_f, b_all, v_all, g_all

    def level_operands(vals):
        q_all, k_all, log_f, b_all, v_all, _ = vals
        q_bf, k_bf, v_bf = q_all.astype(BF16), k_all.astype(BF16), v_all.astype(BF16)
        operands = {n: _hgrn_level_operands(n, q_bf, k_bf, b_all, log_f, row, signs) for n in levels}
        b_last = b_all[c - 1:c, :]
        q_dec = q_bf * jnp.exp2(b_all.astype(BF16))
        k_dec = k_bf * jnp.exp2((b_last - b_all).astype(BF16))
        if two_sided:
            b3 = b_all.reshape(c // HG_BLOCK, HG_BLOCK, HG_WIDTH)
            ref = jnp.concatenate([jnp.zeros((1, 1, HG_WIDTH), F32), b3[:-1, HG_BLOCK - 1:, :]], axis=0)
            rel = (b3 - ref).reshape(c, HG_WIDTH)
            within = (q_bf * jnp.exp2(rel).astype(BF16), k_bf * jnp.exp2(-rel).astype(BF16))
        else:
            within = q_all * k_all
        return operands, q_dec, k_dec, v_bf, jnp.exp2(b_last), within

    def scores_and_readout(vals, ops):
        v_all = vals[4]
        operands, q_dec, k_dec, v_bf, st_decay, within = ops
        tiles = [[None] * (c // SUBLANES) for _ in heads]
        for n in levels:
            q_n, k_n = operands[n]
            for h, sl in enumerate(heads):
                _hgrn_merge_level(n, tiles[h], _dot_nt(q_n[:, sl], k_n[:, sl]), lvl_tiles)
        if two_sided:
            for h, sl in enumerate(heads):
                s_blk = _dot_nt(within[0][:, sl], within[1][:, sl])
                for t, mask in enumerate(block_tiles):
                    s_tile = s_blk[t * SUBLANES:(t + 1) * SUBLANES]
                    old = jnp.zeros_like(s_tile) if tiles[h][t] is None else tiles[h][t]
                    tiles[h][t] = jnp.where(mask, s_tile, old)
        outs = []
        for h, sl in enumerate(heads):
            st = st_ref[h]
            o_h = _dot(jnp.concatenate(tiles[h], axis=0).astype(BF16), v_bf[:, sl])
            o_h = o_h + _dot_nt(q_dec[:, sl], st.astype(BF16))
            st_ref[h] = st * st_decay[:, sl] + _dot_tn(v_bf[:, sl], k_dec[:, sl])
            if not two_sided:
                o_h = o_h + jnp.sum(within[:, sl], axis=-1, keepdims=True) * v_all[:, sl]
            outs.append(o_h)
        return outs

    def finish(ci, x, g_all, outs):
        rows_c = slice(ci * c, (ci + 1) * c)
        for h, sl in enumerate(heads):
            ms = jnp.mean(outs[h] * outs[h], axis=-1, keepdims=True)
            y_ref[rows_c, sl] = (outs[h] * lax.rsqrt(ms + RMS_EPS) * gain * g_all[:, sl]).astype(BF16)
        mix = _dot(y_ref[rows_c, :], w_out)
        o_ref[0, rows_c, :] = _layer_norm(DEEPNORM_ALPHA * x + mix, lng_ref[...], lnb_ref[...])

    n_chunks = x_ref.shape[1] // c
    project_group(0)
    vals = gates(projs[0])
    for ci in range(n_chunks):
        ops = level_operands(vals)
        if ci % HG_PROJ_CHUNKS == 0 and ci + HG_PROJ_CHUNKS < n_chunks:
            project_group(ci // HG_PROJ_CHUNKS + 1, after=vals[3])
        outs = scores_and_readout(vals, ops)
        g_cur = vals[5]
        if ci + 1 < n_chunks:
            vals = gates(projs[ci + 1])
        finish(ci, xs[ci], g_cur, outs)


def _const_spec(shape, layer=None):
    if layer is None:
        zeros = (0,) * len(shape)
        return pl.BlockSpec(shape, lambda *_: zeros, pipeline_mode=pl.Buffered(1))
    zeros = (0,) * (len(shape) - 1)
    return pl.BlockSpec((None,) + tuple(shape[1:]), lambda *_: (layer,) + zeros, pipeline_mode=pl.Buffered(1))


def _hgrn_layer(two_sided, x, w_in, lb, gain, w_out, lng, lnb):
    bsz, seq, d = x.shape
    c = HG_ROWS
    return pl.pallas_call(
        functools.partial(_hgrn_layer_kernel, two_sided),
        grid=(bsz, seq // c),
        in_specs=[
            pl.BlockSpec((1, c, d), lambda b, i: (b, i, 0)),
            _const_spec(w_in.shape, 0), _const_spec(lb.shape), _const_spec(gain.shape),
            _const_spec(w_out.shape, 0), _const_spec(lng.shape), _const_spec(lnb.shape),
        ],
        out_specs=pl.BlockSpec((1, c, d), lambda b, i: (b, i, 0)),
        out_shape=jax.ShapeDtypeStruct(x.shape, F32),
        scratch_shapes=[pltpu.VMEM((HG_HEADS, HG_DK, HG_DK), F32), pltpu.VMEM((c, HG_WIDTH), BF16)],
        compiler_params=pltpu.CompilerParams(dimension_semantics=("arbitrary", "arbitrary"),
                                             vmem_limit_bytes=VMEM_LIMIT_BYTES),
        name="hgrn_layer",
    )(x, w_in, lb, gain, w_out, lng, lnb)


def _ffn_ple_kernel(x_ref, p_ref, wgu_ref, wd_ref, wpu_ref, wpg_ref, bpg_ref, lng_ref, lnb_ref, o_ref):
    x = x_ref[...]
    xb = x.astype(BF16)
    halves = [slice(lo, hi) for lo, hi in FFN_TAIL_ROWS]
    downs = None
    for gi, (lo, hi) in enumerate(FFN_HIDDEN_GROUPS):
        cols, up_cols = slice(lo, hi), slice(FFN_HIDDEN + lo, FFN_HIDDEN + hi)
        gate = _dot(xb, wgu_ref[:, cols].astype(BF16))
        up = _dot(xb, wgu_ref[:, up_cols].astype(BF16))
        hid = (gate * _sigmoid(gate) * up).astype(BF16)
        wd = wd_ref[cols, :].astype(BF16)
        if gi + 1 < len(FFN_HIDDEN_GROUPS):
            part = _dot(hid, wd)
            parts = [part[r] for r in halves]
        else:
            parts = [_dot(hid[r], wd) for r in halves]
        downs = parts if downs is None else [a + b for a, b in zip(downs, parts)]
    ple_up = _dot(p_ref[...].astype(BF16), wpu_ref[...].astype(BF16))
    wpg = wpg_ref[...].astype(BF16)
    for i, r in enumerate(halves):
        x1 = _layer_norm(DEEPNORM_ALPHA * x[r] + downs[i], lng_ref[0:1, :], lnb_ref[0:1, :])
        lhs = _schedule_after(x1, downs[i + 1][-SUBLANES:, :LANES]) if i + 1 < len(halves) else x1
        ple_gate = _sigmoid(_dot(lhs.astype(BF16), wpg) + bpg_ref[...])
        o_ref[r, :] = _layer_norm(DEEPNORM_ALPHA * x1 + ple_gate * ple_up[r], lng_ref[1:2, :], lnb_ref[1:2, :])


def _ffn_ple(layer, x2d, p3d, wgu, wd, wpu, wpg, bpg, lng, lnb):
    t, d = x2d.shape
    rows = FFN_ROWS
    return pl.pallas_call(
        _ffn_ple_kernel,
        grid=(t // rows,),
        in_specs=[
            pl.BlockSpec((rows, d), lambda i: (i, 0)),
            pl.BlockSpec((None, rows, PLE_DIM), lambda i: (layer, i, 0)),
            _const_spec(wgu.shape, layer), _const_spec(wd.shape, layer), _const_spec(wpu.shape, layer),
            _const_spec(wpg.shape, layer), _const_spec(bpg.shape), _const_spec(lng.shape), _const_spec(lnb.shape),
        ],
        out_specs=pl.BlockSpec((rows, d), lambda i: (i, 0)),
        out_shape=jax.ShapeDtypeStruct(x2d.shape, F32),
        compiler_params=pltpu.CompilerParams(dimension_semantics=("arbitrary",),
                                             vmem_limit_bytes=VMEM_LIMIT_BYTES),
        name="ffn_ple",
    )(x2d, p3d, wgu, wd, wpu, wpg, bpg, lng, lnb)


def _swa_layer_kernel(steps_per_seq, slopes_ref, sinks_ref, x_ref, wkv_ref, bkv_ref, wq_ref, bq_ref, wo_ref, bo_ref,
                      lng_ref, lnb_ref, o_ref, kprev_ref, vprev_ref, att_ref, bias_ref, pre_ref):
    step = pl.program_id(0)
    flush_step = pl.num_programs(0) - 1

    def finish_previous():
        o_ref[0] = _layer_norm(pre_ref[...], lng_ref[...], lnb_ref[...])

    @pl.when(step == flush_step)
    def _():
        finish_previous()

    @pl.when(step < flush_step)
    def _():
        _swa_block_step(steps_per_seq, slopes_ref, sinks_ref, x_ref, wkv_ref, bkv_ref, wq_ref, bq_ref, wo_ref,
                        bo_ref, kprev_ref, vprev_ref, att_ref, bias_ref, pre_ref, finish_previous)


def _swa_block_step(steps_per_seq, slopes_ref, sinks_ref, x_ref, wkv_ref, bkv_ref, wq_ref, bq_ref, wo_ref,
                    bo_ref, kprev_ref, vprev_ref, att_ref, bias_ref, pre_ref, finish_previous):
    w = WINDOW
    rows = x_ref.shape[1]
    step = pl.program_id(0)
    first = step % steps_per_seq == 0

    @pl.when(step == 0)
    def _():
        pre_ref[...] = jnp.zeros_like(pre_ref)
        qi = lax.broadcasted_iota(jnp.int32, (w, 2 * w), 0)
        si = lax.broadcasted_iota(jnp.int32, (w, 2 * w), 1)
        dist = qi - si + w
        band = (dist >= 0) & (dist < w)
        dist_f = dist.astype(F32)
        prev_keys = (si >= 1) & (si < w)
        for kh in range(ATT_KVH):
            for par in range(2):
                for r in range(2):
                    head = kh * ATT_G + 2 * r + par
                    bias = jnp.where(band, (slopes_ref[head] * dist_f) * (-LOG2E), -jnp.inf)
                    bias = jnp.where(si == 0, jnp.full((w, 2 * w), sinks_ref[head], F32) * LOG2E, bias)
                    bias_ref[0, kh, par, r * w:(r + 1) * w, :] = bias
                    bias_ref[1, kh, par, r * w:(r + 1) * w, :] = jnp.where(prev_keys, -jnp.inf, bias)

    @pl.when(first)
    def _():
        kprev_ref[...] = jnp.zeros_like(kprev_ref)
        vprev_ref[...] = jnp.zeros_like(vprev_ref)

    finish_previous()
    x = x_ref[0]
    xb = x.astype(BF16)
    kv = _dot(xb, wkv_ref[...].astype(BF16)) + bkv_ref[...]
    kd = kv[:, :ATT_KVH * LANES].astype(BF16)
    vd = kv[:, ATT_KVH * LANES:].astype(BF16)
    q = ((_dot(xb, wq_ref[...].astype(BF16)) + bq_ref[...]) * (ATT_HD ** -0.5 * LOG2E)).astype(BF16)

    lo_half = lax.broadcasted_iota(jnp.int32, (1, LANES), 1) < ATT_HD
    not_row0 = lax.broadcasted_iota(jnp.int32, (BF16_ROWS, 1), 0) > 0
    zero_bf = jnp.zeros((), BF16)
    first_idx = first.astype(jnp.int32)

    for j in range(rows // w):
        blk = slice(j * w, (j + 1) * w)
        if j == 0:
            k_prev, v_prev, tbl = kprev_ref[...], vprev_ref[...], first_idx
        else:
            prev = slice((j - 1) * w, j * w)
            k_prev, v_prev, tbl = kd[prev], vd[prev], 0
        keys = jnp.concatenate([jnp.where(not_row0, k_prev[:BF16_ROWS], zero_bf), k_prev[BF16_ROWS:], kd[blk]], axis=0)
        vals = jnp.concatenate([jnp.where(not_row0, v_prev[:BF16_ROWS], zero_bf), v_prev[BF16_ROWS:], vd[blk]], axis=0)
        for kh in range(ATT_KVH):
            grp = slice(kh * LANES, (kh + 1) * LANES)
            k_dup, v_dup = keys[:, grp], vals[:, grp]
            c0 = kh * ATT_G * ATT_HD
            q2 = jnp.concatenate([q[blk, c0:c0 + LANES], q[blk, c0 + LANES:c0 + 2 * LANES]], axis=0)
            out = None
            for par in range(2):
                keep = lo_half if par == 0 else jnp.logical_not(lo_half)
                s = _dot_nt(q2, jnp.where(keep, k_dup, zero_bf)) + bias_ref[tbl, kh, par]
                p = jnp.exp2(s - jnp.max(s, axis=-1, keepdims=True))
                inv = 1.0 / jnp.sum(p, axis=-1, keepdims=True)
                pv = _dot(p.astype(BF16), jnp.where(keep, v_dup, zero_bf)) * inv
                out = pv if out is None else out + pv
            att_ref[blk, c0:c0 + LANES] = out[:w].astype(BF16)
            att_ref[blk, c0 + LANES:c0 + 2 * LANES] = out[w:].astype(BF16)

    kprev_ref[...] = kd[rows - w:]
    vprev_ref[...] = vd[rows - w:]
    mix = _dot(att_ref[...], wo_ref[...].astype(BF16)) + bo_ref[...]
    pre_ref[...] = DEEPNORM_ALPHA * x + mix


def _swa_layer(x, slopes, sinks, wkv, bkv, wq, bq, wo, bo, lng, lnb):
    bsz, seq, d = x.shape
    rows = SWA_ROWS
    steps = seq // rows
    last = bsz * steps - 1
    smem = pl.BlockSpec(memory_space=pltpu.SMEM)
    return pl.pallas_call(
        functools.partial(_swa_layer_kernel, steps),
        grid=(bsz * steps + 1,),
        in_specs=[
            smem, smem,
            pl.BlockSpec((1, rows, d), lambda i: (jnp.minimum(i, last) // steps, jnp.minimum(i, last) % steps, 0)),
            _const_spec(wkv.shape), _const_spec(bkv.shape), _const_spec(wq.shape, 0), _const_spec(bq.shape),
            _const_spec(wo.shape, 0), _const_spec(bo.shape), _const_spec(lng.shape), _const_spec(lnb.shape),
        ],
        out_specs=pl.BlockSpec((1, rows, d),
                               lambda i: (jnp.maximum(i - 1, 0) // steps, jnp.maximum(i - 1, 0) % steps, 0)),
        out_shape=jax.ShapeDtypeStruct(x.shape, F32),
        scratch_shapes=[pltpu.VMEM((WINDOW, ATT_KVH * LANES), BF16), pltpu.VMEM((WINDOW, ATT_KVH * LANES), BF16),
                        pltpu.VMEM((rows, ATT_QH * ATT_HD), BF16),
                        pltpu.VMEM((2, ATT_KVH, 2, 2 * WINDOW, 2 * WINDOW), F32),
                        pltpu.VMEM((rows, d), F32)],
        compiler_params=pltpu.CompilerParams(dimension_semantics=("arbitrary",),
                                             vmem_limit_bytes=VMEM_LIMIT_BYTES),
        name="swa_layer",
    )(slopes, sinks, x, wkv, bkv, wq, bq, wo, bo, lng, lnb)


def _dup_heads(w):
    lead = w.shape[:-1]
    w4 = w.reshape(lead + (ATT_KVH, 1, ATT_HD))
    return jnp.broadcast_to(w4, lead + (ATT_KVH, 2, ATT_HD)).reshape(lead + (ATT_KVH * 2 * ATT_HD,))


def kernel(x, p, a_w_in, a_lower_bound, a_norm_gain, a_w_out, kv_w, kv_b, b_w_q, b_b_q, b_sinks, b_w_out, b_b_out,
           ffn_w_gate_up, ffn_w_down, ple_w_up, ple_w_gate, ple_b_gate, ln_gain, ln_bias):
    bsz, seq, d = x.shape
    row = lambda v: v.reshape(1, -1).astype(F32)
    lower_bounds = jnp.cumsum(jax.nn.softmax(a_lower_bound.astype(F32), axis=0), axis=0)
    p3d = p.reshape(DEPTH, bsz * seq, PLE_DIM)

    def ffn_ple(i, h):
        out = _ffn_ple(i, h.reshape(bsz * seq, d), p3d, ffn_w_gate_up, ffn_w_down, ple_w_up, ple_w_gate,
                       row(ple_b_gate[i]), ln_gain[i, 1:3], ln_bias[i, 1:3])
        return out.reshape(bsz, seq, d)

    hgrn_args = (x, a_w_in, row(lower_bounds[0]), row(a_norm_gain[0]), a_w_out, row(ln_gain[0, 0]), row(ln_bias[0, 0]))
    h = lax.cond(jnp.min(lower_bounds[0]) >= HG_TWO_SIDED_MIN_LB,
                 functools.partial(_hgrn_layer, True), functools.partial(_hgrn_layer, False), *hgrn_args)
    h = ffn_ple(0, h)

    kdim = ATT_KVH * ATT_HD
    wkv = jnp.concatenate([_dup_heads(kv_w[:, :kdim]), _dup_heads(kv_w[:, kdim:])], axis=-1)
    bkv = row(jnp.concatenate([_dup_heads(kv_b[:kdim]), _dup_heads(kv_b[kdim:])], axis=-1))
    slopes = jnp.exp2(-8.0 * jnp.arange(1, ATT_QH + 1, dtype=F32) / ATT_QH)
    h = _swa_layer(h, slopes, b_sinks[0].astype(F32), wkv, bkv, b_w_q, row(b_b_q[0]),
                   b_w_out, row(b_b_out[0]), row(ln_gain[1, 0]), row(ln_bias[1, 0]))
    h = ffn_ple(1, h)
    return h
```

```python
import functools

import jax
import jax.numpy as jnp
from jax import lax
from jax.experimental import pallas as pl
from jax.experimental.pallas import tpu as pltpu

F32 = jnp.float32
BF16 = jnp.bfloat16

D_MODEL = 1024
DEPTH = 2
HG_DK = 128
HG_HEADS = 8
HG_WIDTH = HG_HEADS * HG_DK
ATT_HD = 64
ATT_QH = 16
ATT_KVH = 4
ATT_G = ATT_QH // ATT_KVH
WINDOW = 128
FFN_HIDDEN = 2816
PLE_DIM = 256
DEEPNORM_ALPHA = (2.0 * DEPTH) ** 0.25
LN_EPS = 1e-5
RMS_EPS = 1e-6

LANES = 128
SUBLANES = 8
BF16_ROWS = 16
LOG2E = 1.4426950408889634
VMEM_LIMIT_BYTES = 56 * 1024 * 1024

HG_CHUNK = 128
HG_ROWS = 512
HG_PROJ_CHUNKS = 2
HG_LEVELS = (64, 32, 16, 8, 4, 2, 1)
HG_BLOCK = BF16_ROWS
HG_TWO_SIDED_MIN_LB = 2.0 ** -6
FFN_ROWS = 512
MXU_WIDTH = 256
FFN_HIDDEN_GROUPS = ((0, 6 * MXU_WIDTH), (6 * MXU_WIDTH, FFN_HIDDEN))
FFN_TAIL_ROWS = ((0, FFN_ROWS // 2), (FFN_ROWS // 2, FFN_ROWS))
SWA_ROWS = 256


def _dot(a, b):
    return jnp.dot(a, b, preferred_element_type=F32)


def _dot_nt(a, b):
    return lax.dot_general(a, b, (((1,), (1,)), ((), ())), preferred_element_type=F32)


def _dot_tn(a, b):
    return lax.dot_general(a, b, (((0,), (0,)), ((), ())), preferred_element_type=F32)


def _sigmoid(x):
    return 1.0 / (1.0 + jnp.exp(-x))


def _sigmoid_approx(x):
    return pl.reciprocal(1.0 + jnp.exp(-x), approx=True)


def _schedule_after(x, dep):
    corner = jnp.where(dep > jnp.inf, dep, x[:SUBLANES, :LANES])
    return jnp.concatenate([jnp.concatenate([corner, x[:SUBLANES, LANES:]], axis=1), x[SUBLANES:]], axis=0)


def _layer_norm(x, g, b):
    mu = jnp.mean(x, axis=-1, keepdims=True)
    xc = x - mu
    var = jnp.mean(xc * xc, axis=-1, keepdims=True)
    return xc * lax.rsqrt(var + LN_EPS) * g + b


def _hgrn_level_exponent(n, b, lf, row, sign):
    c, width = b.shape
    if n >= 4:
        b3 = b.reshape(c // (2 * n), 2 * n, width)
        d = (b3 - b3[:, n - 1:n, :]).reshape(c, width)
        return d * sign
    if n == 2:
        p = row & 3
        nxt = pltpu.roll(lf, c - 1, axis=0)
        prv = pltpu.roll(lf, 1, axis=0)
        zero = jnp.zeros_like(lf)
        return (jnp.where(p == 0, nxt, zero) + jnp.where(p >= 2, lf, zero)
                + jnp.where(p == 3, prv, zero))
    return jnp.where((row & 1) == 1, lf, jnp.zeros_like(lf))


def _hgrn_level_operands(n, q_bf, k_bf, b, lf, row, signs):
    c = b.shape[0]
    if n < BF16_ROWS:
        e = jnp.exp2(_hgrn_level_exponent(n, b, lf, row, signs.get(n)).astype(BF16))
        return q_bf * e, k_bf * e
    q_parts, k_parts = [], []
    for g in range(c // (2 * n)):
        lo = slice(2 * n * g, 2 * n * g + n)
        up = slice(2 * n * g + n, 2 * n * (g + 1))
        ref_row = b[2 * n * g + n - 1:2 * n * g + n, :]
        q_parts.append(q_bf[up] * jnp.exp2((b[up] - ref_row).astype(BF16)))
        k_parts += [k_bf[lo] * jnp.exp2((ref_row - b[lo]).astype(BF16)), k_bf[up]]
    return jnp.concatenate(q_parts, axis=0), jnp.concatenate(k_parts, axis=0)


def _hgrn_merge_level(n, tiles, s_n, lvl_tiles):
    level = n.bit_length() - 1

    def merge(t, s_tile):
        old = jnp.zeros_like(s_tile) if tiles[t] is None else tiles[t]
        tiles[t] = jnp.where(lvl_tiles[t] == level, s_tile, old)

    if n >= BF16_ROWS:
        per = n // SUBLANES
        for i in range(s_n.shape[0] // SUBLANES):
            merge(2 * per * (i // per) + per + i % per, s_n[i * SUBLANES:(i + 1) * SUBLANES])
    else:
        for t in range(len(tiles)):
            if n == SUBLANES and t % 2 == 0:
                continue
            merge(t, s_n[t * SUBLANES:(t + 1) * SUBLANES])


def _hgrn_layer_kernel(two_sided, x_ref, w_in_ref, lb_ref, gain_ref, w_out_ref, lng_ref, lnb_ref,
                       o_ref, st_ref, y_ref):
    c = HG_CHUNK
    levels = tuple(n for n in HG_LEVELS if n >= HG_BLOCK) if two_sided else HG_LEVELS

    @pl.when(pl.program_id(1) == 0)
    def _():
        st_ref[...] = jnp.zeros_like(st_ref)

    w_in = w_in_ref[...].astype(BF16)
    w_out = w_out_ref[...].astype(BF16)
    lb = lb_ref[...]
    one_m_lb = 1.0 - lb
    gain = gain_ref[...]

    rr = lax.broadcasted_iota(jnp.int32, (c, c), 0)
    cc = lax.broadcasted_iota(jnp.int32, (c, c), 1)
    tri = (rr >= cc).astype(BF16)
    xr = rr ^ cc
    lvl = jnp.full((c, c), -1, jnp.int32)
    for j in range(len(HG_LEVELS)):
        lvl = lvl + (xr >= (1 << j)).astype(jnp.int32)
    lvl = jnp.where(rr > cc, lvl, -1)
    lvl_tiles = [lvl[t * SUBLANES:(t + 1) * SUBLANES] for t in range(c // SUBLANES)]
    in_block = ((rr // HG_BLOCK) == (cc // HG_BLOCK)) & (rr >= cc)
    block_tiles = [in_block[t * SUBLANES:(t + 1) * SUBLANES] for t in range(c // SUBLANES)]
    row = lax.broadcasted_iota(jnp.int32, (c, HG_WIDTH), 0)
    signs = {n: jnp.where(((row // n) & 1) == 1, 1.0, -1.0).astype(F32) for n in HG_LEVELS if 4 <= n < BF16_ROWS}
    heads = [slice(h * HG_DK, (h + 1) * HG_DK) for h in range(HG_HEADS)]

    xs, projs = {}, {}

    def project_group(gi, after=None):
        m = HG_PROJ_CHUNKS
        xg = x_ref[0, m * gi * c:m * (gi + 1) * c, :]
        lhs = xg if after is None else _schedule_after(xg, after[c - SUBLANES:, :LANES])
        proj_g = _dot(lhs.astype(BF16), w_in)
        for k in range(m):
            xs[m * gi + k], projs[m * gi + k] = xg[k * c:(k + 1) * c], proj_g[k * c:(k + 1) * c]

    def gates(proj):
        gated = one_m_lb * _sigmoid(proj[:, HG_WIDTH:2 * HG_WIDTH])
        log_f = jnp.log2(lb + gated)
        k_all = one_m_lb - gated
        q_raw = proj[:, :HG_WIDTH]
        q_all = q_raw * _sigmoid_approx(q_raw) * (HG_DK ** -0.5)
        v_all = proj[:, 2 * HG_WIDTH:3 * HG_WIDTH]
        g_raw = proj[:, 3 * HG_WIDTH:]
        g_all = g_raw * _sigmoid_approx(g_raw)
        hi = log_f.astype(BF16)
        lo = (log_f - hi.astype(F32)).astype(BF16)
        b_all = _dot(tri, hi) + _dot(tri, lo)
        return q_all, k_all, log_f, b_all, v_all, g_all

    def level_operands(vals):
        q_all, k_all, log_f, b_all, v_all, _ = vals
        q_bf, k_bf, v_bf = q_all.astype(BF16), k_all.astype(BF16), v_all.astype(BF16)
        operands = {n: _hgrn_level_operands(n, q_bf, k_bf, b_all, log_f, row, signs) for n in levels}
        b_last = b_all[c - 1:c, :]
        q_dec = q_bf * jnp.exp2(b_all.astype(BF16))
        k_dec = k_bf * jnp.exp2((b_last - b_all).astype(BF16))
        if two_sided:
            b3 = b_all.reshape(c // HG_BLOCK, HG_BLOCK, HG_WIDTH)
            ref = jnp.concatenate([jnp.zeros((1, 1, HG_WIDTH), F32), b3[:-1, HG_BLOCK - 1:, :]], axis=0)
            rel = (b3 - ref).reshape(c, HG_WIDTH)
            within = (q_bf * jnp.exp2(rel).astype(BF16), k_bf * jnp.exp2(-rel).astype(BF16))
        else:
            within = q_all * k_all
        return operands, q_dec, k_dec, v_bf, jnp.exp2(b_last), within

    def scores_and_readout(vals, ops):
        v_all = vals[4]
        operands, q_dec, k_dec, v_bf, st_decay, within = ops
        tiles = [[None] * (c // SUBLANES) for _ in heads]
        for n in levels:
            q_n, k_n = operands[n]
            for h, sl in enumerate(heads):
                _hgrn_merge_level(n, tiles[h], _dot_nt(q_n[:, sl], k_n[:, sl]), lvl_tiles)
        if two_sided:
            for h, sl in enumerate(heads):
                s_blk = _dot_nt(within[0][:, sl], within[1][:, sl])
                for t, mask in enumerate(block_tiles):
                    s_tile = s_blk[t * SUBLANES:(t + 1) * SUBLANES]
                    old = jnp.zeros_like(s_tile) if tiles[h][t] is None else tiles[h][t]
                    tiles[h][t] = jnp.where(mask, s_tile, old)
        outs = []
        for h, sl in enumerate(heads):
            st = st_ref[h]
            o_h = _dot(jnp.concatenate(tiles[h], axis=0).astype(BF16), v_bf[:, sl])
            o_h = o_h + _dot_nt(q_dec[:, sl], st.astype(BF16))
            st_ref[h] = st * st_decay[:, sl] + _dot_tn(v_bf[:, sl], k_dec[:, sl])
            if not two_sided:
                o_h = o_h + jnp.sum(within[:, sl], axis=-1, keepdims=True) * v_all[:, sl]
            outs.append(o_h)
        return outs

    def finish(ci, x, g_all, outs):
        rows_c = slice(ci * c, (ci + 1) * c)
        for h, sl in enumerate(heads):
            ms = jnp.mean(outs[h] * outs[h], axis=-1, keepdims=True)
            y_ref[rows_c, sl] = (outs[h] * lax.rsqrt(ms + RMS_EPS) * gain * g_all[:, sl]).astype(BF16)
        mix = _dot(y_ref[rows_c, :], w_out)
        o_ref[0, rows_c, :] = _layer_norm(DEEPNORM_ALPHA * x + mix, lng_ref[...], lnb_ref[...])

    n_chunks = x_ref.shape[1] // c
    project_group(0)
    vals = gates(projs[0])
    for ci in range(n_chunks):
        ops = level_operands(vals)
        if ci % HG_PROJ_CHUNKS == 0 and ci + HG_PROJ_CHUNKS < n_chunks:
            project_group(ci // HG_PROJ_CHUNKS + 1, after=vals[3])
        outs = scores_and_readout(vals, ops)
        g_cur = vals[5]
        if ci + 1 < n_chunks:
            vals = gates(projs[ci + 1])
        finish(ci, xs[ci], g_cur, outs)


def _const_spec(shape, layer=None):
    if layer is None:
        zeros = (0,) * len(shape)
        return pl.BlockSpec(shape, lambda *_: zeros, pipeline_mode=pl.Buffered(1))
    zeros = (0,) * (len(shape) - 1)
    return pl.BlockSpec((None,) + tuple(shape[1:]), lambda *_: (layer,) + zeros, pipeline_mode=pl.Buffered(1))


def _hgrn_layer(two_sided, x, w_in, lb, gain, w_out, lng, lnb):
    bsz, seq, d = x.shape
    c = HG_ROWS
    return pl.pallas_call(
        functools.partial(_hgrn_layer_kernel, two_sided),
        grid=(bsz, seq // c),
        in_specs=[
            pl.BlockSpec((1, c, d), lambda b, i: (b, i, 0)),
            _const_spec(w_in.shape, 0), _const_spec(lb.shape), _const_spec(gain.shape),
            _const_spec(w_out.shape, 0), _const_spec(lng.shape), _const_spec(lnb.shape),
        ],
        out_specs=pl.BlockSpec((1, c, d), lambda b, i: (b, i, 0)),
        out_shape=jax.ShapeDtypeStruct(x.shape, F32),
        scratch_shapes=[pltpu.VMEM((HG_HEADS, HG_DK, HG_DK), F32), pltpu.VMEM((c, HG_WIDTH), BF16)],
        compiler_params=pltpu.CompilerParams(dimension_semantics=("arbitrary", "arbitrary"),
                                             vmem_limit_bytes=VMEM_LIMIT_BYTES),
        name="hgrn_layer",
    )(x, w_in, lb, gain, w_out, lng, lnb)


def _ffn_ple_kernel(x_ref, p_ref, wgu_ref, wd_ref, wpu_ref, wpg_ref, bpg_ref, lng_ref, lnb_ref, o_ref):
    x = x_ref[...]
    xb = x.astype(BF16)
    halves = [slice(lo, hi) for lo, hi in FFN_TAIL_ROWS]
    downs = None
    for gi, (lo, hi) in enumerate(FFN_HIDDEN_GROUPS):
        cols, up_cols = slice(lo, hi), slice(FFN_HIDDEN + lo, FFN_HIDDEN + hi)
        gate = _dot(xb, wgu_ref[:, cols].astype(BF16))
        up = _dot(xb, wgu_ref[:, up_cols].astype(BF16))
        hid = (gate * _sigmoid(gate) * up).astype(BF16)
        wd = wd_ref[cols, :].astype(BF16)
        if gi + 1 < len(FFN_HIDDEN_GROUPS):
            part = _dot(hid, wd)
            parts = [part[r] for r in halves]
        else:
            parts = [_dot(hid[r], wd) for r in halves]
        downs = parts if downs is None else [a + b for a, b in zip(downs, parts)]
    ple_up = _dot(p_ref[...].astype(BF16), wpu_ref[...].astype(BF16))
    wpg = wpg_ref[...].astype(BF16)
    for i, r in enumerate(halves):
        x1 = _layer_norm(DEEPNORM_ALPHA * x[r] + downs[i], lng_ref[0:1, :], lnb_ref[0:1, :])
        lhs = _schedule_after(x1, downs[i + 1][-SUBLANES:, :LANES]) if i + 1 < len(halves) else x1
        ple_gate = _sigmoid_approx(_dot(lhs.astype(BF16), wpg) + bpg_ref[...])
        o_ref[r, :] = _layer_norm(DEEPNORM_ALPHA * x1 + ple_gate * ple_up[r], lng_ref[1:2, :], lnb_ref[1:2, :])


def _ffn_ple(layer, x2d, p3d, wgu, wd, wpu, wpg, bpg, lng, lnb):
    t, d = x2d.shape
    rows = FFN_ROWS
    return pl.pallas_call(
        _ffn_ple_kernel,
        grid=(t // rows,),
        in_specs=[
            pl.BlockSpec((rows, d), lambda i: (i, 0)),
            pl.BlockSpec((None, rows, PLE_DIM), lambda i: (layer, i, 0)),
            _const_spec(wgu.shape, layer), _const_spec(wd.shape, layer), _const_spec(wpu.shape, layer),
            _const_spec(wpg.shape, layer), _const_spec(bpg.shape), _const_spec(lng.shape), _const_spec(lnb.shape),
        ],
        out_specs=pl.BlockSpec((rows, d), lambda i: (i, 0)),
        out_shape=jax.ShapeDtypeStruct(x2d.shape, F32),
        compiler_params=pltpu.CompilerParams(dimension_semantics=("arbitrary",),
                                             vmem_limit_bytes=VMEM_LIMIT_BYTES),
        name="ffn_ple",
    )(x2d, p3d, wgu, wd, wpu, wpg, bpg, lng, lnb)


def _swa_layer_kernel(steps_per_seq, slopes_ref, sinks_ref, x_ref, wkv_ref, bkv_ref, wq_ref, bq_ref, wo_ref, bo_ref,
                      lng_ref, lnb_ref, o_ref, kprev_ref, vprev_ref, att_ref, bias_ref, pre_ref):
    step = pl.program_id(0)
    flush_step = pl.num_programs(0) - 1

    def finish_previous():
        o_ref[0] = _layer_norm(pre_ref[...], lng_ref[...], lnb_ref[...])

    @pl.when(step == flush_step)
    def _():
        finish_previous()

    @pl.when(step < flush_step)
    def _():
        _swa_block_step(steps_per_seq, slopes_ref, sinks_ref, x_ref, wkv_ref, bkv_ref, wq_ref, bq_ref, wo_ref,
                        bo_ref, kprev_ref, vprev_ref, att_ref, bias_ref, pre_ref, finish_previous)


def _swa_block_step(steps_per_seq, slopes_ref, sinks_ref, x_ref, wkv_ref, bkv_ref, wq_ref, bq_ref, wo_ref,
                    bo_ref, kprev_ref, vprev_ref, att_ref, bias_ref, pre_ref, finish_previous):
    w = WINDOW
    rows = x_ref.shape[1]
    step = pl.program_id(0)
    first = step % steps_per_seq == 0

    @pl.when(step == 0)
    def _():
        pre_ref[...] = jnp.zeros_like(pre_ref)
        qi = lax.broadcasted_iota(jnp.int32, (w, 2 * w), 0)
        si = lax.broadcasted_iota(jnp.int32, (w, 2 * w), 1)
        dist = qi - si + w
        band = (dist >= 0) & (dist < w)
        dist_f = dist.astype(F32)
        prev_keys = (si >= 1) & (si < w)
        for kh in range(ATT_KVH):
            for par in range(2):
                for r in range(2):
                    head = kh * ATT_G + 2 * r + par
                    bias = jnp.where(band, (slopes_ref[head] * dist_f) * (-LOG2E), -jnp.inf)
                    bias = jnp.where(si == 0, jnp.full((w, 2 * w), sinks_ref[head], F32) * LOG2E, bias)
                    bias_ref[0, kh, par, r * w:(r + 1) * w, :] = bias
                    bias_ref[1, kh, par, r * w:(r + 1) * w, :] = jnp.where(prev_keys, -jnp.inf, bias)

    @pl.when(first)
    def _():
        kprev_ref[...] = jnp.zeros_like(kprev_ref)
        vprev_ref[...] = jnp.zeros_like(vprev_ref)

    finish_previous()
    x = x_ref[0]
    xb = x.astype(BF16)
    kv = _dot(xb, wkv_ref[...].astype(BF16)) + bkv_ref[...]
    kd = kv[:, :ATT_KVH * LANES].astype(BF16)
    vd = kv[:, ATT_KVH * LANES:].astype(BF16)
    q = ((_dot(xb, wq_ref[...].astype(BF16)) + bq_ref[...]) * (ATT_HD ** -0.5 * LOG2E)).astype(BF16)

    lo_half = lax.broadcasted_iota(jnp.int32, (1, LANES), 1) < ATT_HD
    not_row0 = lax.broadcasted_iota(jnp.int32, (BF16_ROWS, 1), 0) > 0
    zero_bf = jnp.zeros((), BF16)
    first_idx = first.astype(jnp.int32)

    for j in range(rows // w):
        blk = slice(j * w, (j + 1) * w)
        if j == 0:
            k_prev, v_prev, tbl = kprev_ref[...], vprev_ref[...], first_idx
        else:
            prev = slice((j - 1) * w, j * w)
            k_prev, v_prev, tbl = kd[prev], vd[prev], 0
        keys = jnp.concatenate([jnp.where(not_row0, k_prev[:BF16_ROWS], zero_bf), k_prev[BF16_ROWS:], kd[blk]], axis=0)
        vals = jnp.concatenate([jnp.where(not_row0, v_prev[:BF16_ROWS], zero_bf), v_prev[BF16_ROWS:], vd[blk]], axis=0)
        for kh in range(ATT_KVH):
            grp = slice(kh * LANES, (kh + 1) * LANES)
            k_dup, v_dup = keys[:, grp], vals[:, grp]
            c0 = kh * ATT_G * ATT_HD
            q2 = jnp.concatenate([q[blk, c0:c0 + LANES], q[blk, c0 + LANES:c0 + 2 * LANES]], axis=0)
            out = None
            for par in range(2):
                keep = lo_half if par == 0 else jnp.logical_not(lo_half)
                s = _dot_nt(q2, jnp.where(keep, k_dup, zero_bf)) + bias_ref[tbl, kh, par]
                p = jnp.exp2(s - jnp.max(s, axis=-1, keepdims=True))
                inv = 1.0 / jnp.sum(p, axis=-1, keepdims=True)
                pv = _dot(p.astype(BF16), jnp.where(keep, v_dup, zero_bf)) * inv
                out = pv if out is None else out + pv
            att_ref[blk, c0:c0 + LANES] = out[:w].astype(BF16)
            att_ref[blk, c0 + LANES:c0 + 2 * LANES] = out[w:].astype(BF16)

    kprev_ref[...] = kd[rows - w:]
    vprev_ref[...] = vd[rows - w:]
    mix = _dot(att_ref[...], wo_ref[...].astype(BF16)) + bo_ref[...]
    pre_ref[...] = DEEPNORM_ALPHA * x + mix


def _swa_layer(x, slopes, sinks, wkv, bkv, wq, bq, wo, bo, lng, lnb):
    bsz, seq, d = x.shape
    rows = SWA_ROWS
    steps = seq // rows
    last = bsz * steps - 1
    smem = pl.BlockSpec(memory_space=pltpu.SMEM)
    return pl.pallas_call(
        functools.partial(_swa_layer_kernel, steps),
        grid=(bsz * steps + 1,),
        in_specs=[
            smem, smem,
            pl.BlockSpec((1, rows, d), lambda i: (jnp.minimum(i, last) // steps, jnp.minimum(i, last) % steps, 0)),
            _const_spec(wkv.shape), _const_spec(bkv.shape), _const_spec(wq.shape, 0), _const_spec(bq.shape),
            _const_spec(wo.shape, 0), _const_spec(bo.shape), _const_spec(lng.shape), _const_spec(lnb.shape),
        ],
        out_specs=pl.BlockSpec((1, rows, d),
                               lambda i: (jnp.maximum(i - 1, 0) // steps, jnp.maximum(i - 1, 0) % steps, 0)),
        out_shape=jax.ShapeDtypeStruct(x.shape, F32),
        scratch_shapes=[pltpu.VMEM((WINDOW, ATT_KVH * LANES), BF16), pltpu.VMEM((WINDOW, ATT_KVH * LANES), BF16),
                        pltpu.VMEM((rows, ATT_QH * ATT_HD), BF16),
                        pltpu.VMEM((2, ATT_KVH, 2, 2 * WINDOW, 2 * WINDOW), F32),
                        pltpu.VMEM((rows, d), F32)],
        compiler_params=pltpu.CompilerParams(dimension_semantics=("arbitrary",),
                                             vmem_limit_bytes=VMEM_LIMIT_BYTES),
        name="swa_layer",
    )(slopes, sinks, x, wkv, bkv, wq, bq, wo, bo, lng, lnb)


def _dup_heads(w):
    lead = w.shape[:-1]
    w4 = w.reshape(lead + (ATT_KVH, 1, ATT_HD))
    return jnp.broadcast_to(w4, lead + (ATT_KVH, 2, ATT_HD)).reshape(lead + (ATT_KVH * 2 * ATT_HD,))


def kernel(x, p, a_w_in, a_lower_bound, a_norm_gain, a_w_out, kv_w, kv_b, b_w_q, b_b_q, b_sinks, b_w_out, b_b_out,
           ffn_w_gate_up, ffn_w_down, ple_w_up, ple_w_gate, ple_b_gate, ln_gain, ln_bias):
    bsz, seq, d = x.shape
    row = lambda v: v.reshape(1, -1).astype(F32)
    lower_bounds = jnp.cumsum(jax.nn.softmax(a_lower_bound.astype(F32), axis=0), axis=0)
    p3d = p.reshape(DEPTH, bsz * seq, PLE_DIM)

    def ffn_ple(i, h):
        out = _ffn_ple(i, h.reshape(bsz * seq, d), p3d, ffn_w_gate_up, ffn_w_down, ple_w_up, ple_w_gate,
                       row(ple_b_gate[i]), ln_gain[i, 1:3], ln_bias[i, 1:3])
        return out.reshape(bsz, seq, d)

    hgrn_args = (x, a_w_in, row(lower_bounds[0]), row(a_norm_gain[0]), a_w_out, row(ln_gain[0, 0]), row(ln_bias[0, 0]))
    h = lax.cond(jnp.min(lower_bounds[0]) >= HG_TWO_SIDED_MIN_LB,
                 functools.partial(_hgrn_layer, True), functools.partial(_hgrn_layer, False), *hgrn_args)
    h = ffn_ple(0, h)

    kdim = ATT_KVH * ATT_HD
    wkv = jnp.concatenate([_dup_heads(kv_w[:, :kdim]), _dup_heads(kv_w[:, kdim:])], axis=-1)
    bkv = row(jnp.concatenate([_dup_heads(kv_b[:kdim]), _dup_heads(kv_b[kdim:])], axis=-1))
    slopes = jnp.exp2(-8.0 * jnp.arange(1, ATT_QH + 1, dtype=F32) / ATT_QH)
    h = _swa_layer(h, slopes, b_sinks[0].astype(F32), wkv, bkv, b_w_q, row(b_b_q[0]),
                   b_w_out, row(b_b_out[0]), row(ln_gain[1, 0]), row(ln_bias[1, 0]))
    h = ffn_ple(1, h)
    return h
```
